```python
import jax, jax.numpy as jnp
from jax import lax
import numpy as np

D_MODEL = 2048
BATCH = 2
SEQ = 4096
DEPTH = 4
DEC_BATCH = 128
DEC_SEQ = 4
PAST_LEN = 8192
PAGE_SIZE = 128

N_MIXERS = 3
RMS_EPS = 1e-6
ROPE_THETA = 10000.0
Q_BLOCK = 128
NEG_INF = -1e30
FORCE_SCORE = 1e9

A_PATTERNS = ((128, 1), (512, 4), (2048, 16))
A_GROUPS = len(A_PATTERNS)
A_HEADS = 8
A_HEAD_DIM = 64
A_GROUP_W = A_HEADS * A_HEAD_DIM
A_IN = 3 * A_GROUPS * A_GROUP_W

NSA_HEADS = 16
NSA_KV = 2
NSA_G = NSA_HEADS // NSA_KV
NSA_HEAD_DIM = 128
NSA_Q = NSA_HEADS * NSA_HEAD_DIM
NSA_KVW = 2 * NSA_KV * NSA_HEAD_DIM
NSA_IN = NSA_Q + 3 * NSA_KVW + 3 * NSA_HEADS
CMP_BLOCK = 32
CMP_STRIDE = 16
CMP_HIDDEN = 128
SEL_BLOCK = 64
SEL_TOPK = 16
NSA_WINDOW = 512

MLA_HEADS = 16
Q_LORA = 768
KV_LORA = 512
QK_NOPE = 128
QK_ROPE = 64
V_DIM = 128
MLA_IN = Q_LORA + KV_LORA + QK_ROPE
MLA_SCALE = (QK_NOPE + QK_ROPE) ** -0.5

D_FF = 5632
N_EXPERTS = 8
MOE_TOPK = 2
D_FF_EXPERT = 5632

kernel_name = 'hybrid_dilated_nsa_mla_decoder_step'


def rmsnorm(x, g):
    xf = x.astype(jnp.float32)
    y = xf * lax.rsqrt(jnp.mean(xf * xf, axis=-1, keepdims=True) + RMS_EPS)
    return (y * g.astype(jnp.float32)).astype(x.dtype)


def rope(x, pos):
    half = x.shape[-1] // 2
    inv = ROPE_THETA ** (-jnp.arange(half, dtype=jnp.float32) / half)
    ang = pos.astype(jnp.float32)[:, None] * inv[None, :]
    shape = (pos.shape[0],) + (1,) * (x.ndim - 3) + (half,)
    cos = jnp.cos(ang).reshape(shape)
    sin = jnp.sin(ang).reshape(shape)
    xf = x.astype(jnp.float32)
    x1, x2 = xf[..., :half], xf[..., half:]
    return jnp.concatenate([x1 * cos - x2 * sin, x2 * cos + x1 * sin], axis=-1).astype(x.dtype)


def attend_shared(q, k, v, mask, scale):
    s = jnp.einsum('bqkgd,btkd->bkgqt', q, k, preferred_element_type=jnp.float32) * scale
    s = jnp.where(mask, s, NEG_INF)
    m = jnp.max(s, axis=-1, keepdims=True)
    e = jnp.where(mask, jnp.exp(s - m), 0.0)
    p = e / jnp.maximum(jnp.sum(e, axis=-1, keepdims=True), 1e-30)
    o = jnp.einsum('bkgqt,btkd->bqkgd', p.astype(v.dtype), v)
    return o, p


def attend_gathered(q, k, v, mask, scale):
    s = jnp.einsum('bqkgd,bqknd->bqkgn', q, k, preferred_element_type=jnp.float32) * scale
    mask = mask[..., None, :]
    s = jnp.where(mask, s, NEG_INF)
    m = jnp.max(s, axis=-1, keepdims=True)
    e = jnp.where(mask, jnp.exp(s - m), 0.0)
    den = jnp.maximum(jnp.sum(e, axis=-1, keepdims=True), 1e-30)
    o = jnp.einsum('bqkgn,bqknd->bqkgd', (e / den).astype(v.dtype), v)
    return o, (m + jnp.log(den))[..., 0]


def a_project(h, pos, w_in):
    B, T, _ = h.shape
    z = (h @ w_in).reshape(B, T, 3, A_GROUPS, A_HEADS, A_HEAD_DIM)
    return rope(z[:, :, 0], pos), rope(z[:, :, 1], pos), z[:, :, 2]


def dilated_attend(q, ks, vs, qrows):
    outs, lses = [], []
    for g, (win, dil) in enumerate(A_PATTERNS):
        offs = dil * jnp.arange(win // dil + 1)
        idx = qrows[g][:, None] - offs[None, :]
        valid = idx >= 0
        idx = jnp.maximum(idx, 0)
        kg = jnp.swapaxes(jnp.take(ks[g], idx, axis=1), 2, 3)
        vg = jnp.swapaxes(jnp.take(vs[g], idx, axis=1), 2, 3)
        o, lse = attend_gathered(q[:, :, g, :, None, :], kg, vg, valid[None, :, None, :], A_HEAD_DIM ** -0.5)
        outs.append(o[:, :, :, 0])
        lses.append(lse[..., 0])
    w = jax.nn.softmax(jnp.stack(lses, axis=0), axis=0)
    return sum(w[g][..., None].astype(outs[g].dtype) * outs[g] for g in range(A_GROUPS))


def a_mixer_prompt(h, w_in, w_out):
    B, T, _ = h.shape
    pos = jnp.arange(T)
    q, k, v = a_project(h, pos, w_in)
    ks = [k[:, :, g] for g in range(A_GROUPS)]
    vs = [v[:, :, g] for g in range(A_GROUPS)]
    nb = T // Q_BLOCK
    qb = jnp.swapaxes(q.reshape(B, nb, Q_BLOCK, A_GROUPS, A_HEADS, A_HEAD_DIM), 0, 1)
    o = lax.map(lambda a: dilated_attend(a[0], ks, vs, [a[1]] * A_GROUPS), (qb, pos.reshape(nb, Q_BLOCK)))
    o = jnp.swapaxes(o, 0, 1).reshape(B, T, A_GROUP_W)
    states = [jnp.stack([ks[g], vs[g]], axis=2)[:, T - min(win, T):] for g, (win, _) in enumerate(A_PATTERNS)]
    return o @ w_out, states


def a_mixer_sample(h, bufs, w_in, w_out):
    Bd, S, _ = h.shape
    pos = PAST_LEN + jnp.arange(S)
    q, k, v = a_project(h, pos, w_in)
    ks, vs, rows, states = [], [], [], []
    for g in range(A_GROUPS):
        wb = bufs[g].shape[1]
        full = jnp.concatenate([bufs[g], jnp.stack([k[:, :, g], v[:, :, g]], axis=2)], axis=1)
        ks.append(full[:, :, 0])
        vs.append(full[:, :, 1])
        rows.append(wb + jnp.arange(S))
        states.append(full[:, S:])
    o = dilated_attend(q, ks, vs, rows).reshape(Bd, S, A_GROUP_W)
    return o @ w_out, states


def nsa_project(h, pos, w_in):
    B, T, _ = h.shape
    z = h @ w_in
    q = z[..., :NSA_Q].reshape(B, T, NSA_KV, NSA_G, NSA_HEAD_DIM)
    kv = z[..., NSA_Q:NSA_Q + 3 * NSA_KVW].reshape(B, T, 3, 2, NSA_KV, NSA_HEAD_DIM)
    gates = jax.nn.sigmoid(z[..., NSA_Q + 3 * NSA_KVW:].reshape(B, T, NSA_KV, NSA_G, 3))
    def rot_k(r):
        return jnp.stack([rope(r[:, :, 0], pos), r[:, :, 1]], axis=2)
    return q, rope(q, pos), kv[:, :, 0], rot_k(kv[:, :, 1]), rot_k(kv[:, :, 2]), gates


def cmp_chunk_proj(rows, w_cmp1):
    B, L = rows.shape[:2]
    c = rows.reshape(B, L // CMP_STRIDE, CMP_STRIDE, 2, NSA_KV, NSA_HEAD_DIM)
    w = w_cmp1.reshape(2, CMP_BLOCK // CMP_STRIDE, CMP_STRIDE, NSA_HEAD_DIM, CMP_HIDDEN)
    return jnp.einsum('bcjsnd,shjdf->bcsnhf', c, w)


def cmp_blocks(P, w_cmp1, w_cmp2, pe_cmp):
    bias = jnp.einsum('sjd,sjdf->sf', pe_cmp, w_cmp1)
    hid = jax.nn.gelu(P[:, :-1, :, :, 0] + P[:, 1:, :, :, 1] + bias[:, None, :])
    return jnp.einsum('bcsnf,sfd->bcsnd', hid, w_cmp2)


def nsa_compressed_and_select(q, comp, qpos, total_len):
    n_cmp = comp.shape[1]
    end = CMP_STRIDE * jnp.arange(n_cmp) + CMP_BLOCK - 1
    mask = end[None, :] <= qpos[:, None]
    o_cmp, p = attend_shared(q, comp[:, :, 0], comp[:, :, 1], mask, NSA_HEAD_DIM ** -0.5)
    imp = jnp.sum(p, axis=2)
    n_sel = -(-total_len // SEL_BLOCK)
    r = SEL_BLOCK // CMP_STRIDE
    front = CMP_BLOCK // CMP_STRIDE - 1
    span = front + r
    back = max(0, r * (n_sel - 1) + span - (n_cmp + front))
    imp = jnp.pad(imp, ((0, 0), (0, 0), (0, 0), (front, back)))
    p_slc = sum(imp[..., o:o + r * (n_sel - 1) + 1:r] for o in range(span))
    tb = qpos // SEL_BLOCK
    j = jnp.arange(n_sel)[None, :]
    valid = j <= tb[:, None]
    forced = (j == 0) | (j == tb[:, None]) | (j == tb[:, None] - 1)
    score = jnp.where(valid, jnp.where(forced, FORCE_SCORE, p_slc), NEG_INF)
    top_s, idx = lax.top_k(score, min(SEL_TOPK, n_sel))
    return o_cmp, jnp.swapaxes(idx, 1, 2), jnp.swapaxes(top_s > 0.5 * NEG_INF, 1, 2)


def nsa_selected_attend(q_rot, ks, vs, idx, sel_valid, qpos):
    B, Tq, KV, K = idx.shape
    kpos = idx[..., None] * SEL_BLOCK + jnp.arange(SEL_BLOCK)
    mask = sel_valid[..., None] & (kpos <= qpos[:, None, None, None])
    n = K * SEL_BLOCK
    o, _ = attend_gathered(q_rot, ks.reshape(B, Tq, KV, n, NSA_HEAD_DIM), vs.reshape(B, Tq, KV, n, NSA_HEAD_DIM),
                           mask.reshape(B, Tq, KV, n), NSA_HEAD_DIM ** -0.5)
    return o


def nsa_window_prompt(q_rot, win_rows):
    B, T = q_rot.shape[:2]
    nb = T // Q_BLOCK
    npad = NSA_WINDOW // Q_BLOCK
    rp = jnp.pad(win_rows, ((0, 0), (npad * Q_BLOCK, 0), (0, 0), (0, 0), (0, 0)))
    rb = rp.reshape(B, nb + npad, Q_BLOCK, 2, NSA_KV, NSA_HEAD_DIM)
    nk = (npad + 1) * Q_BLOCK
    band = jnp.concatenate([rb[:, r:r + nb] for r in range(npad + 1)], axis=2).reshape(B * nb, nk, 2, NSA_KV, NSA_HEAD_DIM)
    qb = q_rot.reshape(B * nb, Q_BLOCK, NSA_KV, NSA_G, NSA_HEAD_DIM)
    qpos = jnp.arange(T).reshape(nb, Q_BLOCK)
    kpos = (jnp.arange(nb)[:, None] - npad) * Q_BLOCK + jnp.arange(nk)[None, :]
    dist = qpos[:, :, None] - kpos[:, None, :]
    mask = (dist >= 0) & (dist < NSA_WINDOW) & (kpos[:, None, :] >= 0)
    mask = jnp.broadcast_to(mask[None], (B, nb, Q_BLOCK, nk)).reshape(B * nb, 1, 1, Q_BLOCK, nk)
    o, _ = attend_shared(qb, band[:, :, 0], band[:, :, 1], mask, NSA_HEAD_DIM ** -0.5)
    return o.reshape(B, T, NSA_KV, NSA_G, NSA_HEAD_DIM)


def nsa_merge(gates, o_cmp, o_sel, o_win, w_out):
    o = gates[..., 0:1] * o_cmp + gates[..., 1:2] * o_sel + gates[..., 2:3] * o_win
    B, T = o.shape[:2]
    return o.reshape(B, T, NSA_Q) @ w_out


def nsa_mixer_prompt(h, w_in, w_cmp1, w_cmp2, pe_cmp, w_out):
    B, T, _ = h.shape
    pos = jnp.arange(T)
    q, q_rot, cmp_rows, sel_rows, win_rows, gates = nsa_project(h, pos, w_in)
    comp = cmp_blocks(cmp_chunk_proj(cmp_rows, w_cmp1), w_cmp1, w_cmp2, pe_cmp)
    o_cmp, idx, sel_valid = nsa_compressed_and_select(q, comp, pos, T)
    n_sel = T // SEL_BLOCK
    blk = sel_rows.reshape(B, n_sel, SEL_BLOCK, 2, NSA_KV, NSA_HEAD_DIM).transpose(0, 4, 1, 2, 3, 5)
    bi = jnp.arange(B)[:, None, None, None]
    hi = jnp.arange(NSA_KV)[None, None, :, None]
    nb = T // Q_BLOCK
    def to_blocks(a):
        return jnp.swapaxes(a.reshape((B, nb, Q_BLOCK) + a.shape[2:]), 0, 1)
    def sel_block(args):
        qb, ib, vb, pb = args
        g = blk[bi, hi, ib]
        return nsa_selected_attend(qb, g[..., 0, :], g[..., 1, :], ib, vb, pb)
    o_sel = lax.map(sel_block, (to_blocks(q_rot), to_blocks(idx), to_blocks(sel_valid), pos.reshape(nb, Q_BLOCK)))
    o_sel = jnp.swapaxes(o_sel, 0, 1).reshape(B, T, NSA_KV, NSA_G, NSA_HEAD_DIM)
    o_win = nsa_window_prompt(q_rot, win_rows)
    y = nsa_merge(gates, o_cmp, o_sel, o_win, w_out)
    return y, [cmp_rows, sel_rows, win_rows[:, T - min(NSA_WINDOW, T):]]


def nsa_mixer_sample(h, cache_cmp, cache_sel, win_buf, page_table, w_in, w_cmp1, w_cmp2, pe_cmp, w_out):
    Bd, S, _ = h.shape
    pos = PAST_LEN + jnp.arange(S)
    q, q_rot, cmp_rows, sel_rows, win_rows, gates = nsa_project(h, pos, w_in)
    past_cmp = cache_cmp[page_table].reshape(Bd, PAST_LEN, 2, NSA_KV, NSA_HEAD_DIM)
    P = cmp_chunk_proj(past_cmp, w_cmp1)
    n_new = S // CMP_STRIDE
    if n_new > 0:
        P = jnp.concatenate([P, cmp_chunk_proj(cmp_rows[:, :n_new * CMP_STRIDE], w_cmp1)], axis=1)
    comp = cmp_blocks(P, w_cmp1, w_cmp2, pe_cmp)
    o_cmp, idx, sel_valid = nsa_compressed_and_select(q, comp, pos, PAST_LEN + S)
    per_page = PAGE_SIZE // SEL_BLOCK
    nbp = PAST_LEN // SEL_BLOCK
    nbn = -(-S // SEL_BLOCK)
    pool = cache_sel.reshape(-1, SEL_BLOCK, 2, NSA_KV, NSA_HEAD_DIM)
    bi = jnp.arange(Bd)[:, None, None, None]
    hi = jnp.arange(NSA_KV)[None, None, :, None]
    jp = jnp.minimum(idx, nbp - 1)
    phys = page_table[bi, jp // per_page] * per_page + jp % per_page
    g_past = pool[phys, :, :, hi]
    new_blk = jnp.pad(sel_rows, ((0, 0), (0, nbn * SEL_BLOCK - S), (0, 0), (0, 0), (0, 0)))
    new_blk = new_blk.reshape(Bd, nbn, SEL_BLOCK, 2, NSA_KV, NSA_HEAD_DIM).transpose(0, 4, 1, 2, 3, 5)
    g_new = new_blk[bi, hi, jnp.clip(idx - nbp, 0, nbn - 1)]
    g = jnp.where((idx >= nbp)[..., None, None, None], g_new, g_past)
    o_sel = nsa_selected_attend(q_rot, g[..., 0, :], g[..., 1, :], idx, sel_valid, pos)
    wb = win_buf.shape[1]
    full = jnp.concatenate([win_buf, win_rows], axis=1)
    kpos = PAST_LEN - wb + jnp.arange(wb + S)
    dist = pos[:, None] - kpos[None, :]
    mask = (dist >= 0) & (dist < NSA_WINDOW)
    o_win, _ = attend_shared(q_rot, full[:, :, 0], full[:, :, 1], mask, NSA_HEAD_DIM ** -0.5)
    y = nsa_merge(gates, o_cmp, o_sel, o_win, w_out)
    return y, [cmp_rows, sel_rows, full[:, S:]]


def mla_project(h, pos, w_in, q_norm, kv_norm, w_qb):
    B, T, _ = h.shape
    z = h @ w_in
    cq = rmsnorm(z[..., :Q_LORA], q_norm)
    ckv = rmsnorm(z[..., Q_LORA:Q_LORA + KV_LORA], kv_norm)
    k_rope = rope(z[..., Q_LORA + KV_LORA:], pos)
    qh = (cq @ w_qb).reshape(B, T, MLA_HEADS, QK_NOPE + QK_ROPE)
    return qh[..., :QK_NOPE], rope(qh[..., QK_NOPE:], pos), ckv, k_rope


def mla_mixer_prompt(h, w_in, q_norm, kv_norm, w_qb, w_kvb, w_out):
    B, T, _ = h.shape
    pos = jnp.arange(T)
    q_nope, q_rope, ckv, k_rope = mla_project(h, pos, w_in, q_norm, kv_norm, w_qb)
    kv = (ckv @ w_kvb).reshape(B, T, MLA_HEADS, QK_NOPE + V_DIM)
    k = jnp.concatenate([kv[..., :QK_NOPE], jnp.broadcast_to(k_rope[:, :, None, :], (B, T, MLA_HEADS, QK_ROPE))], axis=-1)
    v = kv[..., QK_NOPE:]
    q = jnp.concatenate([q_nope, q_rope], axis=-1)
    nb = T // Q_BLOCK
    qb = jnp.swapaxes(q.reshape(B, nb, Q_BLOCK, MLA_HEADS, 1, QK_NOPE + QK_ROPE), 0, 1)
    def block(args):
        qblk, qp = args
        return attend_shared(qblk, k, v, qp[:, None] >= pos[None, :], MLA_SCALE)[0]
    o = lax.map(block, (qb, pos.reshape(nb, Q_BLOCK)))
    o = jnp.swapaxes(o, 0, 1).reshape(B, T, MLA_HEADS * V_DIM)
    return o @ w_out, [ckv, k_rope]


def mla_mixer_sample(h, cache_lat, cache_kr, page_table, w_in, q_norm, kv_norm, w_qb, w_kvb, w_out):
    Bd, S, _ = h.shape
    pos = PAST_LEN + jnp.arange(S)
    q_nope, q_rope, ckv, k_rope = mla_project(h, pos, w_in, q_norm, kv_norm, w_qb)
    w_kv = w_kvb.reshape(KV_LORA, MLA_HEADS, QK_NOPE + V_DIM)
    q_abs = jnp.einsum('bshd,chd->bshc', q_nope, w_kv[..., :QK_NOPE])
    lat = cache_lat[page_table].reshape(Bd, PAST_LEN, KV_LORA)
    kr = cache_kr[page_table].reshape(Bd, PAST_LEN, QK_ROPE)
    def scores(c, r):
        return (jnp.einsum('bshc,btc->bhst', q_abs, c, preferred_element_type=jnp.float32)
                + jnp.einsum('bshr,btr->bhst', q_rope, r, preferred_element_type=jnp.float32)) * MLA_SCALE
    causal = pos[:, None] >= pos[None, :]
    s = jnp.concatenate([scores(lat, kr), jnp.where(causal, scores(ckv, k_rope), NEG_INF)], axis=-1)
    p = jax.nn.softmax(s, axis=-1).astype(lat.dtype)
    ctx = jnp.einsum('bhst,btc->bshc', p[..., :PAST_LEN], lat) + jnp.einsum('bhst,btc->bshc', p[..., PAST_LEN:], ckv)
    o = jnp.einsum('bshc,chd->bshd', ctx, w_kv[..., QK_NOPE:]).reshape(Bd, S, MLA_HEADS * V_DIM)
    return o @ w_out, [ckv, k_rope]


def swiglu(x, w_gu, w_down):
    gu = x @ w_gu
    f = gu.shape[-1] // 2
    return (jax.nn.silu(gu[..., :f]) * gu[..., f:]) @ w_down


def moe_swiglu(x, router, w_e_gu, w_e_down):
    shape = x.shape
    xf = x.reshape(-1, shape[-1])
    logits = jnp.dot(xf, router, preferred_element_type=jnp.float32)
    top_v, top_i = lax.top_k(logits, MOE_TOPK)
    gate = jnp.sum(jax.nn.softmax(top_v, axis=-1)[..., None] * jax.nn.one_hot(top_i, N_EXPERTS, dtype=jnp.float32), axis=1)
    y = jnp.zeros_like(xf)
    for e in range(N_EXPERTS):
        y = y + gate[:, e:e + 1].astype(xf.dtype) * swiglu(xf, w_e_gu[e], w_e_down[e])
    return y.reshape(shape)


def setup_inputs(seed: int = 0) -> dict:
    key = jax.random.key(seed)
    keys = jax.random.split(key, 42)
    f32 = jnp.float32
    def rnd(i, shape, scale=1.0):
        return jax.random.normal(keys[i], shape, f32) * scale
    def gain(i, shape):
        return 1.0 + 0.01 * jax.random.normal(keys[i], shape, f32)
    n_pages = PAST_LEN // PAGE_SIZE
    n_used = DEC_BATCH * n_pages
    n_phys = n_used + max(1, n_used // 4)
    page_table = jax.random.permutation(keys[13], n_phys)[:n_used].reshape(DEC_BATCH, n_pages).astype(jnp.int32)
    def a_buf(i, win):
        return rnd(i, (DEC_BATCH, min(win, PAST_LEN), 2, A_HEADS, A_HEAD_DIM))
    return {
        'x_prompt': rnd(0, (BATCH, SEQ, D_MODEL)),
        'x_sample': rnd(1, (DEC_BATCH, DEC_SEQ, D_MODEL)),
        'state_l0_a_g0': a_buf(2, A_PATTERNS[0][0]),
        'state_l0_a_g1': a_buf(3, A_PATTERNS[1][0]),
        'state_l0_a_g2': a_buf(4, A_PATTERNS[2][0]),
        'cache_l1_nsa_cmp': rnd(5, (n_phys, PAGE_SIZE, 2, NSA_KV, NSA_HEAD_DIM)),
        'cache_l1_nsa_sel': rnd(6, (n_phys, PAGE_SIZE, 2, NSA_KV, NSA_HEAD_DIM)),
        'state_l1_nsa_win': rnd(7, (DEC_BATCH, min(NSA_WINDOW, PAST_LEN), 2, NSA_KV, NSA_HEAD_DIM)),
        'cache_l2_mla_latent': rnd(8, (n_phys, PAGE_SIZE, KV_LORA)),
        'cache_l2_mla_krope': rnd(9, (n_phys, PAGE_SIZE, QK_ROPE)),
        'state_l3_a_g0': a_buf(10, A_PATTERNS[0][0]),
        'state_l3_a_g1': a_buf(11, A_PATTERNS[1][0]),
        'state_l3_a_g2': a_buf(12, A_PATTERNS[2][0]),
        'page_table': page_table,
        'attn_norm': gain(14, (DEPTH, D_MODEL)),
        'ffn_norm': gain(15, (DEPTH, D_MODEL)),
        'final_norm': gain(16, (D_MODEL,)),
        'w_in_l0': rnd(17, (D_MODEL, A_IN), D_MODEL ** -0.5),
        'w_out_l0': rnd(18, (A_GROUP_W, D_MODEL), A_GROUP_W ** -0.5),
        'w_in_l1': rnd(19, (D_MODEL, NSA_IN), D_MODEL ** -0.5),
        'w_cmp1_l1': rnd(20, (2, CMP_BLOCK, NSA_HEAD_DIM, CMP_HIDDEN), (CMP_BLOCK * NSA_HEAD_DIM) ** -0.5),
        'w_cmp2_l1': rnd(21, (2, CMP_HIDDEN, NSA_HEAD_DIM), CMP_HIDDEN ** -0.5),
        'pe_cmp_l1': rnd(22, (2, CMP_BLOCK, NSA_HEAD_DIM), 0.1),
        'w_out_l1': rnd(23, (NSA_Q, D_MODEL), NSA_Q ** -0.5),
        'w_in_l2': rnd(24, (D_MODEL, MLA_IN), D_MODEL ** -0.5),
        'q_norm_l2': gain(25, (Q_LORA,)),
        'kv_norm_l2': gain(26, (KV_LORA,)),
        'w_qb_l2': rnd(27, (Q_LORA, MLA_HEADS * (QK_NOPE + QK_ROPE)), Q_LORA ** -0.5),
        'w_kvb_l2': rnd(28, (KV_LORA, MLA_HEADS * (QK_NOPE + V_DIM)), KV_LORA ** -0.5),
        'w_out_l2': rnd(29, (MLA_HEADS * V_DIM, D_MODEL), (MLA_HEADS * V_DIM) ** -0.5),
        'w_in_l3': rnd(30, (D_MODEL, A_IN), D_MODEL ** -0.5),
        'w_out_l3': rnd(31, (A_GROUP_W, D_MODEL), A_GROUP_W ** -0.5),
        'w_gu_l0': rnd(32, (D_MODEL, 2 * D_FF), D_MODEL ** -0.5),
        'w_down_l0': rnd(33, (D_FF, D_MODEL), D_FF ** -0.5),
        'router_l1': rnd(34, (D_MODEL, N_EXPERTS), D_MODEL ** -0.5),
        'w_e_gu_l1': rnd(35, (N_EXPERTS, D_MODEL, 2 * D_FF_EXPERT), D_MODEL ** -0.5),
        'w_e_down_l1': rnd(36, (N_EXPERTS, D_FF_EXPERT, D_MODEL), D_FF_EXPERT ** -0.5),
        'w_gu_l2': rnd(37, (D_MODEL, 2 * D_FF), D_MODEL ** -0.5),
        'w_down_l2': rnd(38, (D_FF, D_MODEL), D_FF ** -0.5),
        'router_l3': rnd(39, (D_MODEL, N_EXPERTS), D_MODEL ** -0.5),
        'w_e_gu_l3': rnd(40, (N_EXPERTS, D_MODEL, 2 * D_FF_EXPERT), D_MODEL ** -0.5),
        'w_e_down_l3': rnd(41, (N_EXPERTS, D_FF_EXPERT, D_MODEL), D_FF_EXPERT ** -0.5),
    }


def reference(x_prompt, x_sample, state_l0_a_g0, state_l0_a_g1, state_l0_a_g2, cache_l1_nsa_cmp, cache_l1_nsa_sel,
              state_l1_nsa_win, cache_l2_mla_latent, cache_l2_mla_krope, state_l3_a_g0, state_l3_a_g1, state_l3_a_g2,
              page_table, attn_norm, ffn_norm, final_norm, w_in_l0, w_out_l0, w_in_l1, w_cmp1_l1, w_cmp2_l1, pe_cmp_l1,
              w_out_l1, w_in_l2, q_norm_l2, kv_norm_l2, w_qb_l2, w_kvb_l2, w_out_l2, w_in_l3, w_out_l3, w_gu_l0, w_down_l0,
              router_l1, w_e_gu_l1, w_e_down_l1, w_gu_l2, w_down_l2, router_l3, w_e_gu_l3, w_e_down_l3):
    a_layers = {0: (w_in_l0, w_out_l0, (state_l0_a_g0, state_l0_a_g1, state_l0_a_g2)),
                3: (w_in_l3, w_out_l3, (state_l3_a_g0, state_l3_a_g1, state_l3_a_g2))}
    dense_layers = {0: (w_gu_l0, w_down_l0), 2: (w_gu_l2, w_down_l2)}
    moe_layers = {1: (router_l1, w_e_gu_l1, w_e_down_l1), 3: (router_l3, w_e_gu_l3, w_e_down_l3)}
    xp, xs = x_prompt, x_sample
    new_state = []
    for i in range(DEPTH):
        hp = rmsnorm(xp, attn_norm[i])
        hs = rmsnorm(xs, attn_norm[i])
        kind = i % N_MIXERS
        if kind == 0:
            w_in, w_out, bufs = a_layers[i]
            yp, st_p = a_mixer_prompt(hp, w_in, w_out)
            ys, st_s = a_mixer_sample(hs, bufs, w_in, w_out)
        elif kind == 1:
            yp, st_p = nsa_mixer_prompt(hp, w_in_l1, w_cmp1_l1, w_cmp2_l1, pe_cmp_l1, w_out_l1)
            ys, st_s = nsa_mixer_sample(hs, cache_l1_nsa_cmp, cache_l1_nsa_sel, state_l1_nsa_win, page_table,
                                        w_in_l1, w_cmp1_l1, w_cmp2_l1, pe_cmp_l1, w_out_l1)
        else:
            yp, st_p = mla_mixer_prompt(hp, w_in_l2, q_norm_l2, kv_norm_l2, w_qb_l2, w_kvb_l2, w_out_l2)
            ys, st_s = mla_mixer_sample(hs, cache_l2_mla_latent, cache_l2_mla_krope, page_table,
                                        w_in_l2, q_norm_l2, kv_norm_l2, w_qb_l2, w_kvb_l2, w_out_l2)
        new_state += st_p + st_s
        xp = xp + yp
        xs = xs + ys
        hp = rmsnorm(xp, ffn_norm[i])
        hs = rmsnorm(xs, ffn_norm[i])
        if i % 2 == 0:
            w_gu, w_down = dense_layers[i]
            xp = xp + swiglu(hp, w_gu, w_down)
            xs = xs + swiglu(hs, w_gu, w_down)
        else:
            router, w_e_gu, w_e_down = moe_layers[i]
            xp = xp + moe_swiglu(hp, router, w_e_gu, w_e_down)
            xs = xs + moe_swiglu(hs, router, w_e_gu, w_e_down)
    y_prompt = rmsnorm(xp, final_norm)
    y_sample = rmsnorm(xs, final_norm)
    return (y_prompt, y_sample, *new_state)
```

```python
import functools

import jax
import jax.numpy as jnp
from jax import lax
from jax.experimental import pallas as pl
from jax.experimental.pallas import tpu as pltpu

D_MODEL = 2048
BATCH = 2
SEQ = 4096
DEPTH = 4
DEC_BATCH = 128
DEC_SEQ = 4
PAST_LEN = 8192
PAGE_SIZE = 128

N_MIXERS = 3
RMS_EPS = 1e-6
ROPE_THETA = 10000.0
Q_BLOCK = 128
NEG_INF = -1e30
FORCE_SCORE = 1e9

A_PATTERNS = ((128, 1), (512, 4), (2048, 16))
A_GROUPS = len(A_PATTERNS)
A_HEADS = 8
A_HEAD_DIM = 64
A_GROUP_W = A_HEADS * A_HEAD_DIM
A_IN = 3 * A_GROUPS * A_GROUP_W

NSA_HEADS = 16
NSA_KV = 2
NSA_G = NSA_HEADS // NSA_KV
NSA_HEAD_DIM = 128
NSA_Q = NSA_HEADS * NSA_HEAD_DIM
NSA_KVW = 2 * NSA_KV * NSA_HEAD_DIM
NSA_IN = NSA_Q + 3 * NSA_KVW + 3 * NSA_HEADS
CMP_BLOCK = 32
CMP_STRIDE = 16
CMP_HIDDEN = 128
SEL_BLOCK = 64
SEL_TOPK = 16
NSA_WINDOW = 512

MLA_HEADS = 16
Q_LORA = 768
KV_LORA = 512
QK_NOPE = 128
QK_ROPE = 64
V_DIM = 128
MLA_IN = Q_LORA + KV_LORA + QK_ROPE
MLA_SCALE = (QK_NOPE + QK_ROPE) ** -0.5

D_FF = 5632
N_EXPERTS = 8
MOE_TOPK = 2

N_PROMPT = BATCH * SEQ
N_SAMPLE = DEC_BATCH * DEC_SEQ
N_TOK = N_PROMPT + N_SAMPLE

V7X_VMEM_LIMIT_BYTES = 56 * 1024 * 1024
LANES = 128

F32 = jnp.float32
BF16 = jnp.bfloat16

ROW_TILE = 544
FF_TILE = 512
MOE_ROW_TILE = 512
MOE_TILES = (MOE_TOPK * N_TOK) // MOE_ROW_TILE + N_EXPERTS


def _params(*semantics):
    return pltpu.CompilerParams(dimension_semantics=semantics, vmem_limit_bytes=V7X_VMEM_LIMIT_BYTES)


def _rms(x, g):
    return x * lax.rsqrt(jnp.mean(x * x, axis=-1, keepdims=True) + RMS_EPS) * g


def _norm_mm_kernel(x_ref, g_ref, w_ref, o_ref, h_ref):
    @pl.when(pl.program_id(1) == 0)
    def _():
        h_ref[...] = _rms(x_ref[...], g_ref[...]).astype(BF16)

    o_ref[...] = jnp.dot(h_ref[...], w_ref[...].astype(BF16), preferred_element_type=F32)


def norm_matmul(x, g, w, tn):
    m, k = x.shape
    n = w.shape[1]
    tm = ROW_TILE
    return pl.pallas_call(
        _norm_mm_kernel,
        grid=(m // tm, pl.cdiv(n, tn)),
        in_specs=[pl.BlockSpec((tm, k), lambda i, j: (i, 0)),
                  pl.BlockSpec((1, k), lambda i, j: (0, 0)),
                  pl.BlockSpec((k, tn), lambda i, j: (0, j))],
        out_specs=pl.BlockSpec((tm, tn), lambda i, j: (i, j)),
        out_shape=jax.ShapeDtypeStruct((m, n), F32),
        scratch_shapes=[pltpu.VMEM((tm, k), BF16)],
        compiler_params=_params("parallel", "arbitrary"),
        name="norm_matmul",
    )(x, g.reshape(1, k), w)


def _mm_res_kernel(x_ref, w_ref, r_ref, o_ref):
    o_ref[...] = r_ref[...] + jnp.dot(x_ref[...].astype(BF16), w_ref[...].astype(BF16),
                                       preferred_element_type=F32)


def matmul_residual(x, w, res, tn=512):
    m, k = x.shape
    n = w.shape[1]
    tm = ROW_TILE
    return pl.pallas_call(
        _mm_res_kernel,
        grid=(m // tm, n // tn),
        in_specs=[pl.BlockSpec((tm, k), lambda i, j: (i, 0)),
                  pl.BlockSpec((k, tn), lambda i, j: (0, j)),
                  pl.BlockSpec((tm, tn), lambda i, j: (i, j))],
        out_specs=pl.BlockSpec((tm, tn), lambda i, j: (i, j)),
        out_shape=jax.ShapeDtypeStruct((m, n), F32),
        compiler_params=_params("parallel", "arbitrary"),
        name="matmul_residual",
    )(x, w, res)


def _mm_kernel(x_ref, w_ref, o_ref):
    o_ref[...] = jnp.dot(x_ref[...].astype(BF16), w_ref[...].astype(BF16), preferred_element_type=F32)


def matmul(x, w, tm, tn):
    m, k = x.shape
    n = w.shape[1]
    return pl.pallas_call(
        _mm_kernel,
        grid=(m // tm, n // tn),
        in_specs=[pl.BlockSpec((tm, k), lambda i, j: (i, 0)),
                  pl.BlockSpec((k, tn), lambda i, j: (0, j))],
        out_specs=pl.BlockSpec((tm, tn), lambda i, j: (i, j)),
        out_shape=jax.ShapeDtypeStruct((m, n), F32),
        compiler_params=_params("parallel", "arbitrary"),
        name="matmul",
    )(x, w)


def _ffn_kernel(x_ref, g_ref, wg_ref, wu_ref, wd_ref, o_ref, h_ref):
    @pl.when(pl.program_id(1) == 0)
    def _():
        x = x_ref[...]
        h_ref[...] = _rms(x, g_ref[...]).astype(BF16)
        o_ref[...] = x

    h = h_ref[...]
    gate = jnp.dot(h, wg_ref[...].astype(BF16), preferred_element_type=F32)
    up = jnp.dot(h, wu_ref[...].astype(BF16), preferred_element_type=F32)
    act = (gate * jax.nn.sigmoid(gate) * up).astype(BF16)
    o_ref[...] += jnp.dot(act, wd_ref[...].astype(BF16), preferred_element_type=F32)


def dense_ffn(x, g, w_gu, w_down):
    m, d = x.shape
    f = w_down.shape[0]
    tm, tf = ROW_TILE, FF_TILE
    nf = f // tf
    return pl.pallas_call(
        _ffn_kernel,
        grid=(m // tm, nf),
        in_specs=[pl.BlockSpec((tm, d), lambda i, j: (i, 0)),
                  pl.BlockSpec((1, d), lambda i, j: (0, 0)),
                  pl.BlockSpec((d, tf), lambda i, j: (0, j)),
                  pl.BlockSpec((d, tf), lambda i, j: (0, nf + j)),
                  pl.BlockSpec((tf, d), lambda i, j: (j, 0))],
        out_specs=pl.BlockSpec((tm, d), lambda i, j: (i, 0)),
        out_shape=jax.ShapeDtypeStruct((m, d), F32),
        scratch_shapes=[pltpu.VMEM((tm, d), BF16)],
        compiler_params=_params("parallel", "arbitrary"),
        name="dense_ffn",
    )(x, g.reshape(1, d), w_gu, w_gu, w_down)


def _norm_router_kernel(x_ref, g_ref, r_ref, h_ref, l_ref):
    h = _rms(x_ref[...], g_ref[...])
    h_ref[...] = h.astype(BF16)
    l_ref[...] = jnp.dot(h, r_ref[...], preferred_element_type=F32, precision=lax.Precision.HIGHEST)


def norm_router(x, g, router):
    m, d = x.shape
    tm = ROW_TILE
    r_pad = jnp.pad(router, ((0, 0), (0, LANES - router.shape[1])))
    return pl.pallas_call(
        _norm_router_kernel,
        grid=(m // tm,),
        in_specs=[pl.BlockSpec((tm, d), lambda i: (i, 0)),
                  pl.BlockSpec((1, d), lambda i: (0, 0)),
                  pl.BlockSpec((d, LANES), lambda i: (0, 0))],
        out_specs=[pl.BlockSpec((tm, d), lambda i: (i, 0)),
                   pl.BlockSpec((tm, LANES), lambda i: (i, 0))],
        out_shape=[jax.ShapeDtypeStruct((m, d), BF16), jax.ShapeDtypeStruct((m, LANES), F32)],
        compiler_params=_params("parallel"),
        name="norm_router",
    )(x, g.reshape(1, d), r_pad)


def _moe_kernel(te_ref, tv_ref, h_ref, gate_ref, wg_ref, wu_ref, wd_ref, o_ref):
    t = pl.program_id(0)
    j = pl.program_id(1)

    @pl.when(j == 0)
    def _():
        o_ref[...] = jnp.zeros_like(o_ref)

    @pl.when(tv_ref[t] == 1)
    def _():
        h = h_ref[...]
        gate = jnp.dot(h, wg_ref[...].astype(BF16), preferred_element_type=F32)
        up = jnp.dot(h, wu_ref[...].astype(BF16), preferred_element_type=F32)
        act = (gate * jax.nn.sigmoid(gate) * up).astype(BF16)
        o_ref[...] += jnp.dot(act, wd_ref[...].astype(BF16), preferred_element_type=F32)

    @pl.when(j == pl.num_programs(1) - 1)
    def _():
        o_ref[...] = o_ref[...] * gate_ref[...]


def moe_grouped_ffn(h_sorted, gate_sorted, tile_expert, tile_valid, w_e_gu, w_e_down):
    p, d = h_sorted.shape
    f = w_e_down.shape[1]
    tm, tf = MOE_ROW_TILE, FF_TILE
    nf = f // tf

    def frozen(j, tv, t):
        return jnp.where(tv[t] == 1, j, nf - 1)

    grid_spec = pltpu.PrefetchScalarGridSpec(
        num_scalar_prefetch=2,
        grid=(p // tm, nf),
        in_specs=[pl.BlockSpec((tm, d), lambda t, j, te, tv: (t, 0)),
                  pl.BlockSpec((tm, 1), lambda t, j, te, tv: (t, 0)),
                  pl.BlockSpec((None, d, tf), lambda t, j, te, tv: (te[t], 0, frozen(j, tv, t))),
                  pl.BlockSpec((None, d, tf), lambda t, j, te, tv: (te[t], 0, nf + frozen(j, tv, t))),
                  pl.BlockSpec((None, tf, d), lambda t, j, te, tv: (te[t], frozen(j, tv, t), 0))],
        out_specs=pl.BlockSpec((tm, d), lambda t, j, te, tv: (t, 0)),
    )
    return pl.pallas_call(
        _moe_kernel,
        grid_spec=grid_spec,
        out_shape=jax.ShapeDtypeStruct((p, d), F32),
        compiler_params=_params("arbitrary", "arbitrary"),
        name="moe_grouped_ffn",
    )(tile_expert, tile_valid, h_sorted, gate_sorted, w_e_gu, w_e_gu, w_e_down)


def moe_ffn(x, g, router, w_e_gu, w_e_down):
    m, d = x.shape
    tm = MOE_ROW_TILE
    h, logits = norm_router(x, g, router)
    top_v, top_i = lax.top_k(logits[:, :N_EXPERTS], MOE_TOPK)
    top_g = jax.nn.softmax(top_v, axis=-1)
    pair_e = top_i.reshape(-1)
    onehot = (pair_e[:, None] == jnp.arange(N_EXPERTS)[None, :]).astype(jnp.int32)
    rank = jnp.take_along_axis(jnp.cumsum(onehot, axis=0) - onehot, pair_e[:, None], axis=1)[:, 0]
    counts = jnp.sum(onehot, axis=0)
    tiles_e = (counts + tm - 1) // tm
    tile_end = jnp.cumsum(tiles_e)
    tile_start = tile_end - tiles_e
    slot = tile_start[pair_e] * tm + rank
    tile_ids = jnp.arange(MOE_TILES)
    tile_expert = jnp.minimum(jnp.sum(tile_ids[:, None] >= tile_end[None, :], axis=1), N_EXPERTS - 1)
    tile_valid = (tile_ids < tile_end[-1]).astype(jnp.int32)
    last_e = tile_expert[jnp.maximum(tile_end[-1] - 1, 0)]
    tile_expert = jnp.where(tile_valid == 1, tile_expert, last_e).astype(jnp.int32)
    n_slots = MOE_TILES * tm
    src = jnp.zeros((n_slots,), jnp.int32).at[slot].set(jnp.arange(MOE_TOPK * m, dtype=jnp.int32) // MOE_TOPK)
    gate_sorted = jnp.zeros((n_slots,), F32).at[slot].set(top_g.reshape(-1))
    h_sorted = h[src]
    y_sorted = moe_grouped_ffn(h_sorted, gate_sorted[:, None], tile_expert, tile_valid, w_e_gu, w_e_down)
    y = y_sorted[slot].reshape(m, MOE_TOPK, d)
    return x + y[:, 0] + y[:, 1]


def rope(x, pos):
    half = x.shape[-1] // 2
    inv = ROPE_THETA ** (-jnp.arange(half, dtype=F32) / half)
    ang = pos.astype(F32)[:, None] * inv[None, :]
    shape = (pos.shape[0],) + (1,) * (x.ndim - 3) + (half,)
    cos = jnp.cos(ang).reshape(shape)
    sin = jnp.sin(ang).reshape(shape)
    x1, x2 = x[..., :half], x[..., half:]
    return jnp.concatenate([x1 * cos - x2 * sin, x2 * cos + x1 * sin], axis=-1)


def rmsnorm(x, g):
    return _rms(x, g)


def attend_shared(q, k, v, mask, scale):
    s = jnp.einsum('bqkgd,btkd->bkgqt', q, k, preferred_element_type=F32) * scale
    s = jnp.where(mask, s, NEG_INF)
    m = jnp.max(s, axis=-1, keepdims=True)
    e = jnp.where(mask, jnp.exp(s - m), 0.0)
    p = e / jnp.maximum(jnp.sum(e, axis=-1, keepdims=True), 1e-30)
    o = jnp.einsum('bkgqt,btkd->bqkgd', p.astype(v.dtype), v)
    return o, p


def attend_gathered(q, k, v, mask, scale):
    s = jnp.einsum('bqkgd,bqknd->bqkgn', q, k, preferred_element_type=F32) * scale
    mask = mask[..., None, :]
    s = jnp.where(mask, s, NEG_INF)
    m = jnp.max(s, axis=-1, keepdims=True)
    e = jnp.where(mask, jnp.exp(s - m), 0.0)
    den = jnp.maximum(jnp.sum(e, axis=-1, keepdims=True), 1e-30)
    o = jnp.einsum('bqkgn,bqknd->bqkgd', (e / den).astype(v.dtype), v)
    return o, (m + jnp.log(den))[..., 0]


def a_split(z, pos):
    B, T, _ = z.shape
    z = z.reshape(B, T, 3, A_GROUPS, A_HEADS, A_HEAD_DIM)
    return rope(z[:, :, 0], pos), rope(z[:, :, 1], pos), z[:, :, 2]


def dilated_attend(q, ks, vs, qrows):
    outs, lses = [], []
    for g, (win, dil) in enumerate(A_PATTERNS):
        offs = dil * jnp.arange(win // dil + 1)
        idx = qrows[g][:, None] - offs[None, :]
        valid = idx >= 0
        idx = jnp.maximum(idx, 0)
        kg = jnp.swapaxes(jnp.take(ks[g], idx, axis=1), 2, 3)
        vg = jnp.swapaxes(jnp.take(vs[g], idx, axis=1), 2, 3)
        o, lse = attend_gathered(q[:, :, g, :, None, :], kg, vg, valid[None, :, None, :], A_HEAD_DIM ** -0.5)
        outs.append(o[:, :, :, 0])
        lses.append(lse[..., 0])
    w = jax.nn.softmax(jnp.stack(lses, axis=0), axis=0)
    return sum(w[g][..., None].astype(outs[g].dtype) * outs[g] for g in range(A_GROUPS))


def a_attend_prompt(z):
    B, T, _ = z.shape
    pos = jnp.arange(T)
    q, k, v = a_split(z, pos)
    ks = [k[:, :, g] for g in range(A_GROUPS)]
    vs = [v[:, :, g] for g in range(A_GROUPS)]
    nb = T // Q_BLOCK
    qb = jnp.swapaxes(q.reshape(B, nb, Q_BLOCK, A_GROUPS, A_HEADS, A_HEAD_DIM), 0, 1)
    o = lax.map(lambda a: dilated_attend(a[0], ks, vs, [a[1]] * A_GROUPS), (qb, pos.reshape(nb, Q_BLOCK)))
    o = jnp.swapaxes(o, 0, 1).reshape(B, T, A_GROUP_W)
    states = [jnp.stack([ks[g], vs[g]], axis=2)[:, T - min(win, T):] for g, (win, _) in enumerate(A_PATTERNS)]
    return o, states


def a_attend_sample(z, bufs):
    Bd, S, _ = z.shape
    pos = PAST_LEN + jnp.arange(S)
    q, k, v = a_split(z, pos)
    ks, vs, rows, states = [], [], [], []
    for g in range(A_GROUPS):
        wb = bufs[g].shape[1]
        full = jnp.concatenate([bufs[g], jnp.stack([k[:, :, g], v[:, :, g]], axis=2)], axis=1)
        ks.append(full[:, :, 0])
        vs.append(full[:, :, 1])
        rows.append(wb + jnp.arange(S))
        states.append(full[:, S:])
    o = dilated_attend(q, ks, vs, rows).reshape(Bd, S, A_GROUP_W)
    return o, states


def nsa_split(z, pos):
    B, T, _ = z.shape
    q = z[..., :NSA_Q].reshape(B, T, NSA_KV, NSA_G, NSA_HEAD_DIM)
    kv = z[..., NSA_Q:NSA_Q + 3 * NSA_KVW].reshape(B, T, 3, 2, NSA_KV, NSA_HEAD_DIM)
    gates = jax.nn.sigmoid(z[..., NSA_Q + 3 * NSA_KVW:].reshape(B, T, NSA_KV, NSA_G, 3))

    def rot_k(r):
        return jnp.stack([rope(r[:, :, 0], pos), r[:, :, 1]], axis=2)
    return q, rope(q, pos), kv[:, :, 0], rot_k(kv[:, :, 1]), rot_k(kv[:, :, 2]), gates


def cmp_chunk_proj(rows, w_cmp1):
    B, L = rows.shape[:2]
    n = B * (L // CMP_STRIDE)
    c = rows.reshape(n, CMP_STRIDE, 2, NSA_KV, NSA_HEAD_DIM)
    w = w_cmp1.reshape(2, CMP_BLOCK // CMP_STRIDE, CMP_STRIDE, NSA_HEAD_DIM, CMP_HIDDEN)
    outs = []
    for s in range(2):
        xs = c[:, :, s].transpose(0, 2, 1, 3).reshape(n * NSA_KV, CMP_STRIDE * NSA_HEAD_DIM)
        ws = w[s].transpose(1, 2, 0, 3).reshape(CMP_STRIDE * NSA_HEAD_DIM, 2 * CMP_HIDDEN)
        outs.append(matmul(xs, ws, 512, 2 * CMP_HIDDEN).reshape(B, L // CMP_STRIDE, NSA_KV, 2, CMP_HIDDEN))
    return jnp.stack(outs, axis=2)


def cmp_blocks(P, w_cmp1, w_cmp2, pe_cmp):
    bias = jnp.einsum('sjd,sjdf->sf', pe_cmp, w_cmp1, precision=lax.Precision.HIGHEST)
    hid = jax.nn.gelu(P[:, :-1, :, :, 0] + P[:, 1:, :, :, 1] + bias[:, None, :])
    return jnp.einsum('bcsnf,sfd->bcsnd', hid, w_cmp2)


def nsa_compressed_and_select(q, comp, qpos, total_len):
    n_cmp = comp.shape[1]
    end = CMP_STRIDE * jnp.arange(n_cmp) + CMP_BLOCK - 1
    mask = end[None, :] <= qpos[:, None]
    o_cmp, p = attend_shared(q, comp[:, :, 0], comp[:, :, 1], mask, NSA_HEAD_DIM ** -0.5)
    imp = jnp.sum(p, axis=2)
    n_sel = -(-total_len // SEL_BLOCK)
    r = SEL_BLOCK // CMP_STRIDE
    front = CMP_BLOCK // CMP_STRIDE - 1
    span = front + r
    back = max(0, r * (n_sel - 1) + span - (n_cmp + front))
    imp = jnp.pad(imp, ((0, 0), (0, 0), (0, 0), (front, back)))
    p_slc = sum(imp[..., o:o + r * (n_sel - 1) + 1:r] for o in range(span))
    tb = qpos // SEL_BLOCK
    j = jnp.arange(n_sel)[None, :]
    valid = j <= tb[:, None]
    forced = (j == 0) | (j == tb[:, None]) | (j == tb[:, None] - 1)
    score = jnp.where(valid, jnp.where(forced, FORCE_SCORE, p_slc), NEG_INF)
    top_s, idx = lax.top_k(score, min(SEL_TOPK, n_sel))
    return o_cmp, jnp.swapaxes(idx, 1, 2), jnp.swapaxes(top_s > 0.5 * NEG_INF, 1, 2)


def nsa_selected_attend(q_rot, ks, vs, idx, sel_valid, qpos):
    B, Tq, KV, K = idx.shape
    kpos = idx[..., None] * SEL_BLOCK + jnp.arange(SEL_BLOCK)
    mask = sel_valid[..., None] & (kpos <= qpos[:, None, None, None])
    n = K * SEL_BLOCK
    o, _ = attend_gathered(q_rot, ks.reshape(B, Tq, KV, n, NSA_HEAD_DIM), vs.reshape(B, Tq, KV, n, NSA_HEAD_DIM),
                           mask.reshape(B, Tq, KV, n), NSA_HEAD_DIM ** -0.5)
    return o


def nsa_window_prompt(q_rot, win_rows):
    B, T = q_rot.shape[:2]
    nb = T // Q_BLOCK
    npad = NSA_WINDOW // Q_BLOCK
    rp = jnp.pad(win_rows, ((0, 0), (npad * Q_BLOCK, 0), (0, 0), (0, 0), (0, 0)))
    rb = rp.reshape(B, nb + npad, Q_BLOCK, 2, NSA_KV, NSA_HEAD_DIM)
    nk = (npad + 1) * Q_BLOCK
    band = jnp.concatenate([rb[:, r:r + nb] for r in range(npad + 1)], axis=2).reshape(B * nb, nk, 2, NSA_KV, NSA_HEAD_DIM)
    qb = q_rot.reshape(B * nb, Q_BLOCK, NSA_KV, NSA_G, NSA_HEAD_DIM)
    qpos = jnp.arange(T).reshape(nb, Q_BLOCK)
    kpos = (jnp.arange(nb)[:, None] - npad) * Q_BLOCK + jnp.arange(nk)[None, :]
    dist = qpos[:, :, None] - kpos[:, None, :]
    mask = (dist >= 0) & (dist < NSA_WINDOW) & (kpos[:, None, :] >= 0)
    mask = jnp.broadcast_to(mask[None], (B, nb, Q_BLOCK, nk)).reshape(B * nb, 1, 1, Q_BLOCK, nk)
    o, _ = attend_shared(qb, band[:, :, 0], band[:, :, 1], mask, NSA_HEAD_DIM ** -0.5)
    return o.reshape(B, T, NSA_KV, NSA_G, NSA_HEAD_DIM)


def nsa_merge(gates, o_cmp, o_sel, o_win):
    o = gates[..., 0:1] * o_cmp + gates[..., 1:2] * o_sel + gates[..., 2:3] * o_win
    B, T = o.shape[:2]
    return o.reshape(B, T, NSA_Q)


def nsa_attend_prompt(z, w_cmp1, w_cmp2, pe_cmp):
    B, T, _ = z.shape
    pos = jnp.arange(T)
    q, q_rot, cmp_rows, sel_rows, win_rows, gates = nsa_split(z, pos)
    comp = cmp_blocks(cmp_chunk_proj(cmp_rows, w_cmp1), w_cmp1, w_cmp2, pe_cmp)
    o_cmp, idx, sel_valid = nsa_compressed_and_select(q, comp, pos, T)
    n_sel = T // SEL_BLOCK
    blk = sel_rows.reshape(B, n_sel, SEL_BLOCK, 2, NSA_KV, NSA_HEAD_DIM).transpose(0, 4, 1, 2, 3, 5)
    bi = jnp.arange(B)[:, None, None, None]
    hi = jnp.arange(NSA_KV)[None, None, :, None]
    nb = T // Q_BLOCK

    def to_blocks(a):
        return jnp.swapaxes(a.reshape((B, nb, Q_BLOCK) + a.shape[2:]), 0, 1)

    def sel_block(args):
        qb, ib, vb, pb = args
        g = blk[bi, hi, ib]
        return nsa_selected_attend(qb, g[..., 0, :], g[..., 1, :], ib, vb, pb)
    o_sel = lax.map(sel_block, (to_blocks(q_rot), to_blocks(idx), to_blocks(sel_valid), pos.reshape(nb, Q_BLOCK)))
    o_sel = jnp.swapaxes(o_sel, 0, 1).reshape(B, T, NSA_KV, NSA_G, NSA_HEAD_DIM)
    o_win = nsa_window_prompt(q_rot, win_rows)
    o = nsa_merge(gates, o_cmp, o_sel, o_win)
    return o, [cmp_rows, sel_rows, win_rows[:, T - min(NSA_WINDOW, T):]]


def nsa_attend_sample(z, cache_cmp, cache_sel, win_buf, page_table, w_cmp1, w_cmp2, pe_cmp):
    Bd, S, _ = z.shape
    pos = PAST_LEN + jnp.arange(S)
    q, q_rot, cmp_rows, sel_rows, win_rows, gates = nsa_split(z, pos)
    past_cmp = cache_cmp[page_table].reshape(Bd, PAST_LEN, 2, NSA_KV, NSA_HEAD_DIM)
    P = cmp_chunk_proj(past_cmp, w_cmp1)
    n_new = S // CMP_STRIDE
    if n_new > 0:
        P = jnp.concatenate([P, cmp_chunk_proj(cmp_rows[:, :n_new * CMP_STRIDE], w_cmp1)], axis=1)
    comp = cmp_blocks(P, w_cmp1, w_cmp2, pe_cmp)
    o_cmp, idx, sel_valid = nsa_compressed_and_select(q, comp, pos, PAST_LEN + S)
    per_page = PAGE_SIZE // SEL_BLOCK
    nbp = PAST_LEN // SEL_BLOCK
    nbn = -(-S // SEL_BLOCK)
    pool = cache_sel.reshape(-1, SEL_BLOCK, 2, NSA_KV, NSA_HEAD_DIM)
    bi = jnp.arange(Bd)[:, None, None, None]
    hi = jnp.arange(NSA_KV)[None, None, :, None]
    jp = jnp.minimum(idx, nbp - 1)
    phys = page_table[bi, jp // per_page] * per_page + jp % per_page
    gp = pool.reshape(pool.shape[0], -1)[phys].reshape(phys.shape + pool.shape[1:])
    g_past = jnp.stack([gp[:, :, n, :, :, :, n, :] for n in range(NSA_KV)], axis=2)
    new_blk = jnp.pad(sel_rows, ((0, 0), (0, nbn * SEL_BLOCK - S), (0, 0), (0, 0), (0, 0)))
    new_blk = new_blk.reshape(Bd, nbn, SEL_BLOCK, 2, NSA_KV, NSA_HEAD_DIM).transpose(0, 4, 1, 2, 3, 5)
    g_new = new_blk[bi, hi, jnp.clip(idx - nbp, 0, nbn - 1)]
    g = jnp.where((idx >= nbp)[..., None, None, None], g_new, g_past)
    o_sel = nsa_selected_attend(q_rot, g[..., 0, :], g[..., 1, :], idx, sel_valid, pos)
    wb = win_buf.shape[1]
    full = jnp.concatenate([win_buf, win_rows], axis=1)
    kpos = PAST_LEN - wb + jnp.arange(wb + S)
    dist = pos[:, None] - kpos[None, :]
    mask = (dist >= 0) & (dist < NSA_WINDOW)
    o_win, _ = attend_shared(q_rot, full[:, :, 0], full[:, :, 1], mask, NSA_HEAD_DIM ** -0.5)
    o = nsa_merge(gates, o_cmp, o_sel, o_win)
    return o, [cmp_rows, sel_rows, full[:, S:]]


def mla_attend_prompt(q_nope, q_rope, ckv, k_rope, w_kvb):
    B, T, _ = ckv.shape
    pos = jnp.arange(T)
    kv = matmul(ckv.reshape(B * T, KV_LORA), w_kvb, 512, 512).reshape(B, T, MLA_HEADS, QK_NOPE + V_DIM)
    k = jnp.concatenate([kv[..., :QK_NOPE], jnp.broadcast_to(k_rope[:, :, None, :], (B, T, MLA_HEADS, QK_ROPE))], axis=-1)
    v = kv[..., QK_NOPE:]
    q = jnp.concatenate([q_nope, q_rope], axis=-1)
    nb = T // Q_BLOCK
    qb = jnp.swapaxes(q.reshape(B, nb, Q_BLOCK, MLA_HEADS, 1, QK_NOPE + QK_ROPE), 0, 1)

    def block(args):
        qblk, qp = args
        return attend_shared(qblk, k, v, qp[:, None] >= pos[None, :], MLA_SCALE)[0]
    o = lax.map(block, (qb, pos.reshape(nb, Q_BLOCK)))
    return jnp.swapaxes(o, 0, 1).reshape(B, T, MLA_HEADS * V_DIM)


def mla_attend_sample(q_nope, q_rope, ckv, k_rope, cache_lat, cache_kr, page_table, w_kvb):
    Bd, S, _ = ckv.shape
    pos = PAST_LEN + jnp.arange(S)
    w_kv = w_kvb.reshape(KV_LORA, MLA_HEADS, QK_NOPE + V_DIM)
    q_abs = jnp.einsum('bshd,chd->bshc', q_nope, w_kv[..., :QK_NOPE])
    lat = cache_lat[page_table].reshape(Bd, PAST_LEN, KV_LORA)
    kr = cache_kr[page_table].reshape(Bd, PAST_LEN, QK_ROPE)

    def scores(c, r):
        return (jnp.einsum('bshc,btc->bhst', q_abs, c, preferred_element_type=F32)
                + jnp.einsum('bshr,btr->bhst', q_rope, r, preferred_element_type=F32)) * MLA_SCALE
    causal = pos[:, None] >= pos[None, :]
    s = jnp.concatenate([scores(lat, kr), jnp.where(causal, scores(ckv, k_rope), NEG_INF)], axis=-1)
    p = jax.nn.softmax(s, axis=-1).astype(lat.dtype)
    ctx = jnp.einsum('bhst,btc->bshc', p[..., :PAST_LEN], lat) + jnp.einsum('bhst,btc->bshc', p[..., PAST_LEN:], ckv)
    return jnp.einsum('bshc,chd->bshd', ctx, w_kv[..., QK_NOPE:]).reshape(Bd, S, MLA_HEADS * V_DIM)


def _split_tokens(z):
    return (z[:N_PROMPT].reshape(BATCH, SEQ, -1), z[N_PROMPT:].reshape(DEC_BATCH, DEC_SEQ, -1))


def _join_tokens(zp, zs):
    return jnp.concatenate([zp.reshape(N_PROMPT, -1), zs.reshape(N_SAMPLE, -1)], axis=0)


def kernel(x_prompt, x_sample, state_l0_a_g0, state_l0_a_g1, state_l0_a_g2, cache_l1_nsa_cmp, cache_l1_nsa_sel, state_l1_nsa_win, cache_l2_mla_latent, cache_l2_mla_krope, state_l3_a_g0, state_l3_a_g1, state_l3_a_g2, page_table, attn_norm, ffn_norm, final_norm, w_in_l0, w_out_l0, w_in_l1, w_cmp1_l1, w_cmp2_l1, pe_cmp_l1, w_out_l1, w_in_l2, q_norm_l2, kv_norm_l2, w_qb_l2, w_kvb_l2, w_out_l2, w_in_l3, w_out_l3, w_gu_l0, w_down_l0, router_l1, w_e_gu_l1, w_e_down_l1, w_gu_l2, w_down_l2, router_l3, w_e_gu_l3, w_e_down_l3):
    a_layers = {0: (w_in_l0, w_out_l0, (state_l0_a_g0, state_l0_a_g1, state_l0_a_g2)),
                3: (w_in_l3, w_out_l3, (state_l3_a_g0, state_l3_a_g1, state_l3_a_g2))}
    dense_layers = {0: (w_gu_l0, w_down_l0), 2: (w_gu_l2, w_down_l2)}
    moe_layers = {1: (router_l1, w_e_gu_l1, w_e_down_l1), 3: (router_l3, w_e_gu_l3, w_e_down_l3)}
    x = _join_tokens(x_prompt, x_sample)
    pos_p = jnp.arange(SEQ)
    pos_s = PAST_LEN + jnp.arange(DEC_SEQ)
    new_state = []
    for i in range(DEPTH):
        kind = i % N_MIXERS
        if kind == 0:
            w_in, w_out, bufs = a_layers[i]
            zp, zs = _split_tokens(norm_matmul(x, attn_norm[i], w_in, 512))
            op, st_p = a_attend_prompt(zp)
            os_, st_s = a_attend_sample(zs, bufs)
        elif kind == 1:
            w_out = w_out_l1
            zp, zs = _split_tokens(norm_matmul(x, attn_norm[i], w_in_l1, 512))
            op, st_p = nsa_attend_prompt(zp, w_cmp1_l1, w_cmp2_l1, pe_cmp_l1)
            os_, st_s = nsa_attend_sample(zs, cache_l1_nsa_cmp, cache_l1_nsa_sel, state_l1_nsa_win, page_table,
                                          w_cmp1_l1, w_cmp2_l1, pe_cmp_l1)
        else:
            w_out = w_out_l2
            z = norm_matmul(x, attn_norm[i], w_in_l2, MLA_IN)
            cq = rmsnorm(z[:, :Q_LORA], q_norm_l2)
            ckv = rmsnorm(z[:, Q_LORA:Q_LORA + KV_LORA], kv_norm_l2)
            qh = matmul(cq, w_qb_l2, ROW_TILE, 512)
            qhp, qhs = _split_tokens(qh)
            ckv_p, ckv_s = _split_tokens(ckv)
            kr_p, kr_s = _split_tokens(z[:, Q_LORA + KV_LORA:])
            kr_p, kr_s = rope(kr_p, pos_p), rope(kr_s, pos_s)
            qhp = qhp.reshape(BATCH, SEQ, MLA_HEADS, QK_NOPE + QK_ROPE)
            qhs = qhs.reshape(DEC_BATCH, DEC_SEQ, MLA_HEADS, QK_NOPE + QK_ROPE)
            op = mla_attend_prompt(qhp[..., :QK_NOPE], rope(qhp[..., QK_NOPE:], pos_p), ckv_p, kr_p, w_kvb_l2)
            os_ = mla_attend_sample(qhs[..., :QK_NOPE], rope(qhs[..., QK_NOPE:], pos_s), ckv_s, kr_s,
                                    cache_l2_mla_latent, cache_l2_mla_krope, page_table, w_kvb_l2)
            st_p, st_s = [ckv_p, kr_p], [ckv_s, kr_s]
        new_state += st_p + st_s
        x = matmul_residual(_join_tokens(op, os_), w_out, x)
        if i % 2 == 0:
            w_gu, w_down = dense_layers[i]
            x = dense_ffn(x, ffn_norm[i], w_gu, w_down)
        else:
            router, w_e_gu, w_e_down = moe_layers[i]
            x = moe_ffn(x, ffn_norm[i], router, w_e_gu, w_e_down)
    y = final_rmsnorm(x, final_norm)
    y_prompt, y_sample = _split_tokens(y)
    return (y_prompt, y_sample, *new_state)


def _final_norm_kernel(x_ref, g_ref, o_ref):
    o_ref[...] = _rms(x_ref[...], g_ref[...])


def final_rmsnorm(x, g):
    m, d = x.shape
    tm = ROW_TILE
    return pl.pallas_call(
        _final_norm_kernel,
        grid=(m // tm,),
        in_specs=[pl.BlockSpec((tm, d), lambda i: (i, 0)), pl.BlockSpec((1, d), lambda i: (0, 0))],
        out_specs=pl.BlockSpec((tm, d), lambda i: (i, 0)),
        out_shape=jax.ShapeDtypeStruct((m, d), F32),
        compiler_params=_params("parallel"),
        name="final_rmsnorm",
    )(x, g.reshape(1, d))
```

```python
import functools

import jax
import jax.numpy as jnp
from jax import lax
from jax.experimental import pallas as pl
from jax.experimental.pallas import tpu as pltpu

D_MODEL = 2048
BATCH = 2
SEQ = 4096
DEPTH = 4
DEC_BATCH = 128
DEC_SEQ = 4
PAST_LEN = 8192
PAGE_SIZE = 128

N_MIXERS = 3
RMS_EPS = 1e-6
ROPE_THETA = 10000.0
Q_BLOCK = 128
NEG_INF = -1e30
FORCE_SCORE = 1e9

A_PATTERNS = ((128, 1), (512, 4), (2048, 16))
A_GROUPS = len(A_PATTERNS)
A_HEADS = 8
A_HEAD_DIM = 64
A_GROUP_W = A_HEADS * A_HEAD_DIM
A_IN = 3 * A_GROUPS * A_GROUP_W

NSA_HEADS = 16
NSA_KV = 2
NSA_G = NSA_HEADS // NSA_KV
NSA_HEAD_DIM = 128
NSA_Q = NSA_HEADS * NSA_HEAD_DIM
NSA_KVW = 2 * NSA_KV * NSA_HEAD_DIM
NSA_IN = NSA_Q + 3 * NSA_KVW + 3 * NSA_HEADS
CMP_BLOCK = 32
CMP_STRIDE = 16
CMP_HIDDEN = 128
SEL_BLOCK = 64
SEL_TOPK = 16
NSA_WINDOW = 512

MLA_HEADS = 16
Q_LORA = 768
KV_LORA = 512
QK_NOPE = 128
QK_ROPE = 64
V_DIM = 128
MLA_IN = Q_LORA + KV_LORA + QK_ROPE
MLA_SCALE = (QK_NOPE + QK_ROPE) ** -0.5

D_FF = 5632
N_EXPERTS = 8
MOE_TOPK = 2

N_PROMPT = BATCH * SEQ
N_SAMPLE = DEC_BATCH * DEC_SEQ
N_TOK = N_PROMPT + N_SAMPLE

V7X_VMEM_LIMIT_BYTES = 56 * 1024 * 1024
LANES = 128

F32 = jnp.float32
BF16 = jnp.bfloat16

ROW_TILE = 544
FF_TILE = 512
MOE_ROW_TILE = 512
MOE_TILES = (MOE_TOPK * N_TOK) // MOE_ROW_TILE + N_EXPERTS


def _params(*semantics):
    return pltpu.CompilerParams(dimension_semantics=semantics, vmem_limit_bytes=V7X_VMEM_LIMIT_BYTES)


def _rms(x, g):
    return x * lax.rsqrt(jnp.mean(x * x, axis=-1, keepdims=True) + RMS_EPS) * g


def rope_tables(pos, head_dim):
    half = head_dim // 2
    inv = ROPE_THETA ** (-jnp.arange(half, dtype=F32) / half)
    ang = pos.astype(F32)[:, None] * inv[None, :]
    cos, sin = jnp.cos(ang), jnp.sin(ang)
    reps = LANES // head_dim
    return (jnp.tile(jnp.concatenate([cos, cos], axis=1), (1, reps)),
            jnp.tile(jnp.concatenate([-sin, sin], axis=1), (1, reps)))


def _rope_lanes(x, cos, sin, head_dim):
    n = x.shape[1]
    half = head_dim // 2
    reps = n // LANES
    if reps > 1:
        cos = jnp.concatenate([cos] * reps, axis=1)
        sin = jnp.concatenate([sin] * reps, axis=1)
    lane = lax.broadcasted_iota(jnp.int32, x.shape, 1) % head_dim
    partner = jnp.where(lane < half, pltpu.roll(x, n - half, 1), pltpu.roll(x, half, 1))
    return x * cos + partner * sin


def _norm_mm_kernel(x_ref, g_ref, w_ref, cos_ref, sin_ref, o_ref, h_ref, *, head_dim, rope_blocks):
    j = pl.program_id(1)

    @pl.when(j == 0)
    def _():
        h_ref[...] = _rms(x_ref[...], g_ref[...]).astype(BF16)

    acc = jnp.dot(h_ref[...], w_ref[...].astype(BF16), preferred_element_type=F32)
    if not rope_blocks:
        o_ref[...] = acc
        return
    is_rope = functools.reduce(jnp.logical_or, [j == b for b in rope_blocks])

    @pl.when(is_rope)
    def _():
        o_ref[...] = _rope_lanes(acc, cos_ref[...], sin_ref[...], head_dim)

    @pl.when(jnp.logical_not(is_rope))
    def _():
        o_ref[...] = acc


def norm_matmul(x, g, w, tn, rope=None):
    m, k = x.shape
    n = w.shape[1]
    tm = ROW_TILE
    if rope is None:
        cos = sin = jnp.zeros((m, LANES), F32)
        head_dim, rope_blocks = LANES, ()
    else:
        cos, sin, head_dim, rope_blocks = rope
    return pl.pallas_call(
        functools.partial(_norm_mm_kernel, head_dim=head_dim, rope_blocks=tuple(rope_blocks)),
        grid=(m // tm, pl.cdiv(n, tn)),
        in_specs=[pl.BlockSpec((tm, k), lambda i, j: (i, 0)),
                  pl.BlockSpec((1, k), lambda i, j: (0, 0)),
                  pl.BlockSpec((k, tn), lambda i, j: (0, j)),
                  pl.BlockSpec((tm, LANES), lambda i, j: (i, 0)),
                  pl.BlockSpec((tm, LANES), lambda i, j: (i, 0))],
        out_specs=pl.BlockSpec((tm, tn), lambda i, j: (i, j)),
        out_shape=jax.ShapeDtypeStruct((m, n), F32),
        scratch_shapes=[pltpu.VMEM((tm, k), BF16)],
        compiler_params=_params("parallel", "arbitrary"),
        name="norm_matmul",
    )(x, g.reshape(1, k), w, cos, sin)


def _mm_res_kernel(x_ref, w_ref, r_ref, o_ref):
    o_ref[...] = r_ref[...] + jnp.dot(x_ref[...].astype(BF16), w_ref[...].astype(BF16),
                                       preferred_element_type=F32)


def matmul_residual(x, w, res, tn=512):
    m, k = x.shape
    n = w.shape[1]
    tm = ROW_TILE
    return pl.pallas_call(
        _mm_res_kernel,
        grid=(m // tm, n // tn),
        in_specs=[pl.BlockSpec((tm, k), lambda i, j: (i, 0)),
                  pl.BlockSpec((k, tn), lambda i, j: (0, j)),
                  pl.BlockSpec((tm, tn), lambda i, j: (i, j))],
        out_specs=pl.BlockSpec((tm, tn), lambda i, j: (i, j)),
        out_shape=jax.ShapeDtypeStruct((m, n), F32),
        compiler_params=_params("parallel", "arbitrary"),
        name="matmul_residual",
    )(x, w, res)


def _mm_kernel(x_ref, w_ref, o_ref):
    o_ref[...] = jnp.dot(x_ref[...].astype(BF16), w_ref[...].astype(BF16), preferred_element_type=F32)


def matmul(x, w, tm, tn):
    m, k = x.shape
    n = w.shape[1]
    return pl.pallas_call(
        _mm_kernel,
        grid=(m // tm, n // tn),
        in_specs=[pl.BlockSpec((tm, k), lambda i, j: (i, 0)),
                  pl.BlockSpec((k, tn), lambda i, j: (0, j))],
        out_specs=pl.BlockSpec((tm, tn), lambda i, j: (i, j)),
        out_shape=jax.ShapeDtypeStruct((m, n), F32),
        compiler_params=_params("parallel", "arbitrary"),
        name="matmul",
    )(x, w)


def _ffn_kernel(x_ref, g_ref, wg_ref, wu_ref, wd_ref, o_ref, h_ref):
    @pl.when(pl.program_id(1) == 0)
    def _():
        x = x_ref[...]
        h_ref[...] = _rms(x, g_ref[...]).astype(BF16)
        o_ref[...] = x

    h = h_ref[...]
    gate = jnp.dot(h, wg_ref[...].astype(BF16), preferred_element_type=F32)
    up = jnp.dot(h, wu_ref[...].astype(BF16), preferred_element_type=F32)
    act = (gate * jax.nn.sigmoid(gate) * up).astype(BF16)
    o_ref[...] += jnp.dot(act, wd_ref[...].astype(BF16), preferred_element_type=F32)


def dense_ffn(x, g, w_gu, w_down):
    m, d = x.shape
    f = w_down.shape[0]
    tm, tf = ROW_TILE, FF_TILE
    nf = f // tf
    return pl.pallas_call(
        _ffn_kernel,
        grid=(m // tm, nf),
        in_specs=[pl.BlockSpec((tm, d), lambda i, j: (i, 0)),
                  pl.BlockSpec((1, d), lambda i, j: (0, 0)),
                  pl.BlockSpec((d, tf), lambda i, j: (0, j)),
                  pl.BlockSpec((d, tf), lambda i, j: (0, nf + j)),
                  pl.BlockSpec((tf, d), lambda i, j: (j, 0))],
        out_specs=pl.BlockSpec((tm, d), lambda i, j: (i, 0)),
        out_shape=jax.ShapeDtypeStruct((m, d), F32),
        scratch_shapes=[pltpu.VMEM((tm, d), BF16)],
        compiler_params=_params("parallel", "arbitrary"),
        name="dense_ffn",
    )(x, g.reshape(1, d), w_gu, w_gu, w_down)


def _norm_router_kernel(x_ref, g_ref, r_ref, h_ref, l_ref):
    h = _rms(x_ref[...], g_ref[...])
    h_ref[...] = h.astype(BF16)
    l_ref[...] = jnp.dot(h, r_ref[...], preferred_element_type=F32, precision=lax.Precision.HIGHEST)


def norm_router(x, g, router):
    m, d = x.shape
    tm = ROW_TILE
    r_pad = jnp.pad(router, ((0, 0), (0, LANES - router.shape[1])))
    return pl.pallas_call(
        _norm_router_kernel,
        grid=(m // tm,),
        in_specs=[pl.BlockSpec((tm, d), lambda i: (i, 0)),
                  pl.BlockSpec((1, d), lambda i: (0, 0)),
                  pl.BlockSpec((d, LANES), lambda i: (0, 0))],
        out_specs=[pl.BlockSpec((tm, d), lambda i: (i, 0)),
                   pl.BlockSpec((tm, LANES), lambda i: (i, 0))],
        out_shape=[jax.ShapeDtypeStruct((m, d), BF16), jax.ShapeDtypeStruct((m, LANES), F32)],
        compiler_params=_params("parallel"),
        name="norm_router",
    )(x, g.reshape(1, d), r_pad)


def _moe_kernel(te_ref, tv_ref, h_ref, gate_ref, wg_ref, wu_ref, wd_ref, o_ref):
    t = pl.program_id(0)
    j = pl.program_id(1)

    @pl.when(j == 0)
    def _():
        o_ref[...] = jnp.zeros_like(o_ref)

    @pl.when(tv_ref[t] == 1)
    def _():
        h = h_ref[...]
        gate = jnp.dot(h, wg_ref[...].astype(BF16), preferred_element_type=F32)
        up = jnp.dot(h, wu_ref[...].astype(BF16), preferred_element_type=F32)
        act = (gate * jax.nn.sigmoid(gate) * up).astype(BF16)
        o_ref[...] += jnp.dot(act, wd_ref[...].astype(BF16), preferred_element_type=F32)

    @pl.when(j == pl.num_programs(1) - 1)
    def _():
        o_ref[...] = o_ref[...] * gate_ref[...]


def moe_grouped_ffn(h_sorted, gate_sorted, tile_expert, tile_valid, w_e_gu, w_e_down):
    p, d = h_sorted.shape
    f = w_e_down.shape[1]
    tm, tf = MOE_ROW_TILE, FF_TILE
    nf = f // tf

    def frozen(j, tv, t):
        return jnp.where(tv[t] == 1, j, nf - 1)

    grid_spec = pltpu.PrefetchScalarGridSpec(
        num_scalar_prefetch=2,
        grid=(p // tm, nf),
        in_specs=[pl.BlockSpec((tm, d), lambda t, j, te, tv: (t, 0)),
                  pl.BlockSpec((tm, 1), lambda t, j, te, tv: (t, 0)),
                  pl.BlockSpec((None, d, tf), lambda t, j, te, tv: (te[t], 0, frozen(j, tv, t))),
                  pl.BlockSpec((None, d, tf), lambda t, j, te, tv: (te[t], 0, nf + frozen(j, tv, t))),
                  pl.BlockSpec((None, tf, d), lambda t, j, te, tv: (te[t], frozen(j, tv, t), 0))],
        out_specs=pl.BlockSpec((tm, d), lambda t, j, te, tv: (t, 0)),
    )
    return pl.pallas_call(
        _moe_kernel,
        grid_spec=grid_spec,
        out_shape=jax.ShapeDtypeStruct((p, d), F32),
        compiler_params=_params("arbitrary", "arbitrary"),
        name="moe_grouped_ffn",
    )(tile_expert, tile_valid, h_sorted, gate_sorted, w_e_gu, w_e_gu, w_e_down)


def moe_ffn(x, g, router, w_e_gu, w_e_down):
    m, d = x.shape
    tm = MOE_ROW_TILE
    h, logits = norm_router(x, g, router)
    top_v, top_i = lax.top_k(logits[:, :N_EXPERTS], MOE_TOPK)
    top_g = jax.nn.softmax(top_v, axis=-1)
    pair_e = top_i.reshape(-1)
    onehot = (pair_e[:, None] == jnp.arange(N_EXPERTS)[None, :]).astype(jnp.int32)
    rank = jnp.take_along_axis(jnp.cumsum(onehot, axis=0) - onehot, pair_e[:, None], axis=1)[:, 0]
    counts = jnp.sum(onehot, axis=0)
    tiles_e = (counts + tm - 1) // tm
    tile_end = jnp.cumsum(tiles_e)
    tile_start = tile_end - tiles_e
    slot = tile_start[pair_e] * tm + rank
    tile_ids = jnp.arange(MOE_TILES)
    tile_expert = jnp.minimum(jnp.sum(tile_ids[:, None] >= tile_end[None, :], axis=1), N_EXPERTS - 1)
    tile_valid = (tile_ids < tile_end[-1]).astype(jnp.int32)
    last_e = tile_expert[jnp.maximum(tile_end[-1] - 1, 0)]
    tile_expert = jnp.where(tile_valid == 1, tile_expert, last_e).astype(jnp.int32)
    n_slots = MOE_TILES * tm
    src = jnp.zeros((n_slots,), jnp.int32).at[slot].set(jnp.arange(MOE_TOPK * m, dtype=jnp.int32) // MOE_TOPK)
    gate_sorted = jnp.zeros((n_slots,), F32).at[slot].set(top_g.reshape(-1))
    h_sorted = h[src]
    y_sorted = moe_grouped_ffn(h_sorted, gate_sorted[:, None], tile_expert, tile_valid, w_e_gu, w_e_down)
    y = y_sorted[slot].reshape(m, MOE_TOPK, d)
    return x + y[:, 0] + y[:, 1]


A_TILE = 128


def _masked_softmax(s, mask):
    s = jnp.where(mask, s, NEG_INF)
    m = jnp.max(s, axis=-1, keepdims=True)
    e = jnp.where(mask, jnp.exp(s - m), 0.0)
    den = jnp.maximum(jnp.sum(e, axis=-1, keepdims=True), 1e-30)
    return e / den, m + jnp.log(den)


def _dilated_attn_kernel(q_ref, kp_ref, ko_ref, vp_ref, vo_ref, o_ref, lse_ref):
    t = pl.program_id(2)
    q = q_ref[0].astype(BF16)
    k = jnp.concatenate([kp_ref[0], ko_ref[0]], axis=0).astype(BF16)
    v = jnp.concatenate([vp_ref[0], vo_ref[0]], axis=0).astype(BF16)
    qi = lax.broadcasted_iota(jnp.int32, (A_TILE, 2 * A_TILE), 0)
    kc = lax.broadcasted_iota(jnp.int32, (A_TILE, 2 * A_TILE), 1)
    dist = qi + A_TILE - kc
    mask = (dist >= 0) & (dist <= A_TILE) & ((kc >= A_TILE) | (t > 0))
    outs, lses = [], []
    for h in range(A_HEADS):
        sl = slice(h * A_HEAD_DIM, (h + 1) * A_HEAD_DIM)
        s = lax.dot_general(q[:, sl], k[:, sl], (((1,), (1,)), ((), ())),
                            preferred_element_type=F32) * (A_HEAD_DIM ** -0.5)
        p, lse = _masked_softmax(s, mask)
        outs.append(jnp.dot(p.astype(BF16), v[:, sl], preferred_element_type=F32))
        lses.append(jnp.broadcast_to(lse, (A_TILE, A_HEAD_DIM)))
    o_ref[0] = jnp.concatenate(outs, axis=1)
    lse_ref[0] = jnp.concatenate(lses, axis=1)


def dilated_attention_prompt(z, g):
    B, T, _ = z.shape
    win, dil = A_PATTERNS[g]
    assert win // dil == A_TILE
    nblk = A_IN // A_GROUP_W
    zr = z.reshape(B, T // dil, dil * A_IN)
    w = A_GROUP_W

    def spec(section, prev):
        def index(b, r, t):
            return (b, jnp.maximum(t - 1, 0) if prev else t, r * nblk + section * A_GROUPS + g)
        return pl.BlockSpec((1, A_TILE, w), index)

    out_spec = pl.BlockSpec((1, A_TILE, w), lambda b, r, t: (b, t, r))
    o, lse = pl.pallas_call(
        _dilated_attn_kernel,
        grid=(B, dil, T // dil // A_TILE),
        in_specs=[spec(0, False), spec(1, True), spec(1, False), spec(2, True), spec(2, False)],
        out_specs=[out_spec, out_spec],
        out_shape=[jax.ShapeDtypeStruct((B, T // dil, dil * w), F32)] * 2,
        compiler_params=_params("parallel", "parallel", "arbitrary"),
        name="dilated_attention_prompt",
    )(zr, zr, zr, zr, zr)
    return o.reshape(B, T, w), lse.reshape(B, T, w)


def _merge_groups_kernel(o0, o1, o2, l0, l1, l2, out_ref):
    ls = [l0[...], l1[...], l2[...]]
    m = jnp.maximum(jnp.maximum(ls[0], ls[1]), ls[2])
    es = [jnp.exp(l - m) for l in ls]
    den = es[0] + es[1] + es[2]
    out_ref[...] = (es[0] / den) * o0[...] + (es[1] / den) * o1[...] + (es[2] / den) * o2[...]


def merge_groups(outs, lses):
    m, w = outs[0].shape
    tm = 512
    spec = pl.BlockSpec((tm, w), lambda i: (i, 0))
    return pl.pallas_call(
        _merge_groups_kernel,
        grid=(m // tm,),
        in_specs=[spec] * 6,
        out_specs=spec,
        out_shape=jax.ShapeDtypeStruct((m, w), F32),
        compiler_params=_params("parallel"),
        name="merge_groups",
    )(*outs, *lses)


NSA_KEY_TILE = 512
NSA_Q_COLS = NSA_G * NSA_HEAD_DIM


def _rope_q_heads(q, cos, sin):
    heads = []
    for g in range(NSA_G):
        qg = q[:, g * NSA_HEAD_DIM:(g + 1) * NSA_HEAD_DIM]
        heads.append((qg * cos + pltpu.roll(qg, NSA_HEAD_DIM // 2, 1) * sin).astype(BF16))
    return heads


def _nsa_sel_kernel(q_ref, cos_ref, sin_ref, mem_ref, k_ref, v_ref, o_ref, qs_ref, m_ref, l_ref, acc_ref):
    qt = pl.program_id(2)
    kt = pl.program_id(3)
    tq, tk, dh = Q_BLOCK, NSA_KEY_TILE, NSA_HEAD_DIM

    @pl.when(kt == 0)
    def _():
        for g, qg in enumerate(_rope_q_heads(q_ref[0], cos_ref[...], sin_ref[...])):
            qs_ref[g * tq:(g + 1) * tq, :] = qg
        m_ref[...] = jnp.full(m_ref.shape, NEG_INF, F32)
        l_ref[...] = jnp.zeros(l_ref.shape, F32)
        acc_ref[...] = jnp.zeros(acc_ref.shape, F32)

    @pl.when(kt * tk <= qt * tq + tq - 1)
    def _():
        k = k_ref[0].astype(BF16)
        v = v_ref[0].astype(BF16)
        n_blk = mem_ref.shape[-1]
        blk = lax.broadcasted_iota(jnp.int32, (n_blk, tk), 0)
        col = lax.broadcasted_iota(jnp.int32, (n_blk, tk), 1)
        expand = (blk == kt * (tk // SEL_BLOCK) + col // SEL_BLOCK).astype(BF16)
        picked = jnp.dot(mem_ref[0, 0], expand, preferred_element_type=F32) > 0.5
        qpos = qt * tq + lax.broadcasted_iota(jnp.int32, (tq, tk), 0)
        kpos = kt * tk + lax.broadcasted_iota(jnp.int32, (tq, tk), 1)
        mask = picked & (kpos <= qpos)
        for g in range(NSA_G):
            rows = slice(g * tq, (g + 1) * tq)
            s = lax.dot_general(qs_ref[rows, :], k, (((1,), (1,)), ((), ())),
                                preferred_element_type=F32) * (dh ** -0.5)
            s = jnp.where(mask, s, NEG_INF)
            m_prev = m_ref[rows, :]
            m_new = jnp.maximum(m_prev, jnp.max(s, axis=-1, keepdims=True))
            alpha = jnp.exp(m_prev - m_new)
            e = jnp.where(mask, jnp.exp(s - m_new[:, :1]), 0.0)
            l_ref[rows, :] = alpha * l_ref[rows, :] + jnp.sum(e, axis=-1, keepdims=True)
            acc_ref[rows, :] = alpha * acc_ref[rows, :] + jnp.dot(e.astype(BF16), v, preferred_element_type=F32)
            m_ref[rows, :] = m_new

    @pl.when(kt == pl.num_programs(3) - 1)
    def _():
        for g in range(NSA_G):
            rows = slice(g * tq, (g + 1) * tq)
            o_ref[0, :, g * dh:(g + 1) * dh] = acc_ref[rows, :] / jnp.maximum(l_ref[rows, :], 1e-30)


def nsa_selected_prompt(z, member, cos, sin):
    B, T, _ = z.shape
    tq, tk, dh = Q_BLOCK, NSA_KEY_TILE, NSA_HEAD_DIM
    k_blk = (NSA_Q + NSA_KVW) // dh
    v_blk = k_blk + NSA_KV

    def last_tile(qt):
        return (qt * tq + tq - 1) // tk

    return pl.pallas_call(
        _nsa_sel_kernel,
        grid=(B, NSA_KV, T // tq, T // tk),
        in_specs=[pl.BlockSpec((1, tq, NSA_Q_COLS), lambda b, n, qt, kt: (b, qt, n)),
                  pl.BlockSpec((tq, LANES), lambda b, n, qt, kt: (qt, 0)),
                  pl.BlockSpec((tq, LANES), lambda b, n, qt, kt: (qt, 0)),
                  pl.BlockSpec((1, 1, tq, T // SEL_BLOCK), lambda b, n, qt, kt: (b, n, qt, 0)),
                  pl.BlockSpec((1, tk, dh), lambda b, n, qt, kt: (b, jnp.minimum(kt, last_tile(qt)), k_blk + n)),
                  pl.BlockSpec((1, tk, dh), lambda b, n, qt, kt: (b, jnp.minimum(kt, last_tile(qt)), v_blk + n))],
        out_specs=pl.BlockSpec((1, tq, NSA_Q_COLS), lambda b, n, qt, kt: (b, qt, n)),
        out_shape=jax.ShapeDtypeStruct((B, T, NSA_Q), F32),
        scratch_shapes=[pltpu.VMEM((NSA_G * tq, dh), BF16), pltpu.VMEM((NSA_G * tq, LANES), F32),
                        pltpu.VMEM((NSA_G * tq, LANES), F32), pltpu.VMEM((NSA_G * tq, dh), F32)],
        compiler_params=_params("parallel", "parallel", "parallel", "arbitrary"),
        name="nsa_selected_prompt",
    )(z, cos, sin, member, z, z)


NSA_WIN_TILES = NSA_WINDOW // Q_BLOCK + 1


def _nsa_win_kernel(q_ref, cos_ref, sin_ref, *refs):
    k_refs, v_refs, o_ref = refs[:NSA_WIN_TILES], refs[NSA_WIN_TILES:2 * NSA_WIN_TILES], refs[-1]
    qt = pl.program_id(2)
    tq, dh = Q_BLOCK, NSA_HEAD_DIM
    nk = NSA_WIN_TILES * tq
    k = jnp.concatenate([r[0] for r in k_refs], axis=0).astype(BF16)
    v = jnp.concatenate([r[0] for r in v_refs], axis=0).astype(BF16)
    qpos = qt * tq + lax.broadcasted_iota(jnp.int32, (tq, nk), 0)
    kpos = (qt - (NSA_WIN_TILES - 1)) * tq + lax.broadcasted_iota(jnp.int32, (tq, nk), 1)
    dist = qpos - kpos
    mask = (dist >= 0) & (dist < NSA_WINDOW) & (kpos >= 0)
    for g, qg in enumerate(_rope_q_heads(q_ref[0], cos_ref[...], sin_ref[...])):
        s = lax.dot_general(qg, k, (((1,), (1,)), ((), ())), preferred_element_type=F32) * (dh ** -0.5)
        p, _ = _masked_softmax(s, mask)
        o_ref[0, :, g * dh:(g + 1) * dh] = jnp.dot(p.astype(BF16), v, preferred_element_type=F32)


def nsa_window_prompt(z, cos, sin):
    B, T, _ = z.shape
    tq, dh = Q_BLOCK, NSA_HEAD_DIM
    k_blk = (NSA_Q + 2 * NSA_KVW) // dh
    v_blk = k_blk + NSA_KV

    def kv_spec(first_blk, i):
        back = NSA_WIN_TILES - 1 - i
        return pl.BlockSpec((1, tq, dh), lambda b, n, qt: (b, jnp.maximum(qt - back, 0), first_blk + n))

    return pl.pallas_call(
        _nsa_win_kernel,
        grid=(B, NSA_KV, T // tq),
        in_specs=[pl.BlockSpec((1, tq, NSA_Q_COLS), lambda b, n, qt: (b, qt, n)),
                  pl.BlockSpec((tq, LANES), lambda b, n, qt: (qt, 0)),
                  pl.BlockSpec((tq, LANES), lambda b, n, qt: (qt, 0))]
                 + [kv_spec(k_blk, i) for i in range(NSA_WIN_TILES)]
                 + [kv_spec(v_blk, i) for i in range(NSA_WIN_TILES)],
        out_specs=pl.BlockSpec((1, tq, NSA_Q_COLS), lambda b, n, qt: (b, qt, n)),
        out_shape=jax.ShapeDtypeStruct((B, T, NSA_Q), F32),
        compiler_params=_params("parallel", "parallel", "arbitrary"),
        name="nsa_window_prompt",
    )(z, cos, sin, *([z] * (2 * NSA_WIN_TILES)))


N_PAGES = PAST_LEN // PAGE_SIZE
CHUNKS_PER_PAGE = PAGE_SIZE // CMP_STRIDE
PAGE_SECTIONS = 2 * NSA_KV


def _cmp_paged_kernel(pt_ref, *refs):
    page_refs, w_ref, o_ref = refs[:N_PAGES], refs[N_PAGES], refs[N_PAGES + 1]
    hc2 = 2 * CMP_HIDDEN
    for s in range(2):
        for n in range(NSA_KV):
            acc = jnp.zeros((N_PAGES * CHUNKS_PER_PAGE, hc2), F32)
            for j in range(CMP_STRIDE):
                first = j * PAGE_SECTIONS + s * NSA_KV + n
                x = jnp.concatenate([r[pl.ds(first, CHUNKS_PER_PAGE, stride=CMP_STRIDE * PAGE_SECTIONS), :]
                                     for r in page_refs], axis=0)
                acc = acc + jnp.dot(x.astype(BF16), w_ref[s, j].astype(BF16), preferred_element_type=F32)
            o_ref[0, :, (s * NSA_KV + n) * hc2:(s * NSA_KV + n + 1) * hc2] = acc


def cmp_chunk_proj_paged(cache_cmp, page_table, w_cmp1):
    n_phys = cache_cmp.shape[0]
    bd = page_table.shape[0]
    pages = cache_cmp.reshape(n_phys * PAGE_SIZE * PAGE_SECTIONS, NSA_HEAD_DIM)
    w = w_cmp1.reshape(2, CMP_BLOCK // CMP_STRIDE, CMP_STRIDE, NSA_HEAD_DIM, CMP_HIDDEN)
    w = w.transpose(0, 2, 3, 1, 4).reshape(2, CMP_STRIDE, NSA_HEAD_DIM, 2 * CMP_HIDDEN)
    n_chunks = N_PAGES * CHUNKS_PER_PAGE
    out_w = 2 * NSA_KV * 2 * CMP_HIDDEN

    def page_spec(i):
        return pl.BlockSpec((PAGE_SIZE * PAGE_SECTIONS, NSA_HEAD_DIM), lambda b, pt: (pt[b, i], 0))

    grid_spec = pltpu.PrefetchScalarGridSpec(
        num_scalar_prefetch=1,
        grid=(bd,),
        in_specs=[page_spec(i) for i in range(N_PAGES)]
                 + [pl.BlockSpec(w.shape, lambda b, pt: (0, 0, 0, 0))],
        out_specs=pl.BlockSpec((1, n_chunks, out_w), lambda b, pt: (b, 0, 0)),
    )
    out = pl.pallas_call(
        _cmp_paged_kernel,
        grid_spec=grid_spec,
        out_shape=jax.ShapeDtypeStruct((bd, n_chunks, out_w), F32),
        compiler_params=_params("arbitrary"),
        name="cmp_chunk_proj_paged",
    )(page_table, *([pages] * N_PAGES), w)
    return out.reshape(bd, n_chunks, 2, NSA_KV, 2, CMP_HIDDEN)


MLA_TQ = 256
MLA_TK = 512
MLA_PAIR = 2


def _mla_prompt_kernel(qn_ref, qr_ref, cos_ref, sin_ref, kv_ref, kr_ref, o_ref, qr_s, m_s, l_s, acc_s):
    qt = pl.program_id(2)
    kt = pl.program_id(3)
    tq, tk = MLA_TQ, MLA_TK

    @pl.when(kt == 0)
    def _():
        qr_s[...] = _rope_lanes(qr_ref[0], cos_ref[...], sin_ref[...], QK_ROPE).astype(BF16)
        m_s[...] = jnp.full(m_s.shape, NEG_INF, F32)
        l_s[...] = jnp.zeros(l_s.shape, F32)
        acc_s[...] = jnp.zeros(acc_s.shape, F32)

    @pl.when(kt * tk <= qt * tq + tq - 1)
    def _():
        kr = kr_ref[0].astype(BF16)
        qpos = qt * tq + lax.broadcasted_iota(jnp.int32, (tq, tk), 0)
        kpos = kt * tk + lax.broadcasted_iota(jnp.int32, (tq, tk), 1)
        mask = qpos >= kpos
        for h in range(MLA_PAIR):
            kn = kv_ref[0, :, h * (QK_NOPE + V_DIM):h * (QK_NOPE + V_DIM) + QK_NOPE].astype(BF16)
            v = kv_ref[0, :, h * (QK_NOPE + V_DIM) + QK_NOPE:(h + 1) * (QK_NOPE + V_DIM)].astype(BF16)
            qn = qn_ref[0, :, h * QK_NOPE:(h + 1) * QK_NOPE].astype(BF16)
            qr = qr_s[:, h * QK_ROPE:(h + 1) * QK_ROPE]
            dims = (((1,), (1,)), ((), ()))
            s = (lax.dot_general(qn, kn, dims, preferred_element_type=F32)
                 + lax.dot_general(qr, kr, dims, preferred_element_type=F32)) * MLA_SCALE
            s = jnp.where(mask, s, NEG_INF)
            m_prev = m_s[h]
            m_new = jnp.maximum(m_prev, jnp.max(s, axis=-1, keepdims=True))
            alpha = jnp.exp(m_prev - m_new)
            e = jnp.where(mask, jnp.exp(s - m_new[:, :1]), 0.0)
            l_s[h] = alpha * l_s[h] + jnp.sum(e, axis=-1, keepdims=True)
            acc_s[h] = alpha * acc_s[h] + jnp.dot(e.astype(BF16), v, preferred_element_type=F32)
            m_s[h] = m_new

    @pl.when(kt == pl.num_programs(3) - 1)
    def _():
        for h in range(MLA_PAIR):
            o_ref[0, :, h * V_DIM:(h + 1) * V_DIM] = acc_s[h] / jnp.maximum(l_s[h], 1e-30)


def mla_attention_prompt(qh, kv, k_rope, cos, sin):
    B, T, _ = kv.shape
    tq, tk = MLA_TQ, MLA_TK
    n_pairs = MLA_HEADS // MLA_PAIR
    rope_blk0 = MLA_HEADS * QK_NOPE // (MLA_PAIR * QK_ROPE)

    def last_tile(qt):
        return (qt * tq + tq - 1) // tk

    return pl.pallas_call(
        _mla_prompt_kernel,
        grid=(B, n_pairs, T // tq, T // tk),
        in_specs=[pl.BlockSpec((1, tq, MLA_PAIR * QK_NOPE), lambda b, p, qt, kt: (b, qt, p)),
                  pl.BlockSpec((1, tq, MLA_PAIR * QK_ROPE), lambda b, p, qt, kt: (b, qt, rope_blk0 + p)),
                  pl.BlockSpec((tq, LANES), lambda b, p, qt, kt: (qt, 0)),
                  pl.BlockSpec((tq, LANES), lambda b, p, qt, kt: (qt, 0)),
                  pl.BlockSpec((1, tk, MLA_PAIR * (QK_NOPE + V_DIM)),
                               lambda b, p, qt, kt: (b, jnp.minimum(kt, last_tile(qt)), p)),
                  pl.BlockSpec((1, tk, QK_ROPE), lambda b, p, qt, kt: (b, jnp.minimum(kt, last_tile(qt)), 0))],
        out_specs=pl.BlockSpec((1, tq, MLA_PAIR * V_DIM), lambda b, p, qt, kt: (b, qt, p)),
        out_shape=jax.ShapeDtypeStruct((B, T, MLA_HEADS * V_DIM), F32),
        scratch_shapes=[pltpu.VMEM((tq, MLA_PAIR * QK_ROPE), BF16), pltpu.VMEM((MLA_PAIR, tq, LANES), F32),
                        pltpu.VMEM((MLA_PAIR, tq, LANES), F32), pltpu.VMEM((MLA_PAIR, tq, V_DIM), F32)],
        compiler_params=_params("parallel", "parallel", "parallel", "arbitrary"),
        name="mla_attention_prompt",
    )(qh, qh, cos, sin, kv, k_rope)


def rope(x, pos):
    half = x.shape[-1] // 2
    inv = ROPE_THETA ** (-jnp.arange(half, dtype=F32) / half)
    ang = pos.astype(F32)[:, None] * inv[None, :]
    shape = (pos.shape[0],) + (1,) * (x.ndim - 3) + (half,)
    cos = jnp.cos(ang).reshape(shape)
    sin = jnp.sin(ang).reshape(shape)
    x1, x2 = x[..., :half], x[..., half:]
    return jnp.concatenate([x1 * cos - x2 * sin, x2 * cos + x1 * sin], axis=-1)


def rmsnorm(x, g):
    return _rms(x, g)


def attend_shared(q, k, v, mask, scale):
    s = jnp.einsum('bqkgd,btkd->bkgqt', q, k, preferred_element_type=F32) * scale
    s = jnp.where(mask, s, NEG_INF)
    m = jnp.max(s, axis=-1, keepdims=True)
    e = jnp.where(mask, jnp.exp(s - m), 0.0)
    p = e / jnp.maximum(jnp.sum(e, axis=-1, keepdims=True), 1e-30)
    o = jnp.einsum('bkgqt,btkd->bqkgd', p.astype(v.dtype), v)
    return o, p


def attend_gathered(q, k, v, mask, scale):
    s = jnp.einsum('bqkgd,bqknd->bqkgn', q, k, preferred_element_type=F32) * scale
    mask = mask[..., None, :]
    s = jnp.where(mask, s, NEG_INF)
    m = jnp.max(s, axis=-1, keepdims=True)
    e = jnp.where(mask, jnp.exp(s - m), 0.0)
    den = jnp.maximum(jnp.sum(e, axis=-1, keepdims=True), 1e-30)
    o = jnp.einsum('bqkgn,bqknd->bqkgd', (e / den).astype(v.dtype), v)
    return o, (m + jnp.log(den))[..., 0]


def a_split(z):
    B, T, _ = z.shape
    z = z.reshape(B, T, 3, A_GROUPS, A_HEADS, A_HEAD_DIM)
    return z[:, :, 0], z[:, :, 1], z[:, :, 2]


def dilated_attend(q, ks, vs, qrows):
    outs, lses = [], []
    for g, (win, dil) in enumerate(A_PATTERNS):
        offs = dil * jnp.arange(win // dil + 1)
        idx = qrows[g][:, None] - offs[None, :]
        valid = idx >= 0
        idx = jnp.maximum(idx, 0)
        kg = jnp.swapaxes(jnp.take(ks[g], idx, axis=1), 2, 3)
        vg = jnp.swapaxes(jnp.take(vs[g], idx, axis=1), 2, 3)
        o, lse = attend_gathered(q[:, :, g, :, None, :], kg, vg, valid[None, :, None, :], A_HEAD_DIM ** -0.5)
        outs.append(o[:, :, :, 0])
        lses.append(lse[..., 0])
    w = jax.nn.softmax(jnp.stack(lses, axis=0), axis=0)
    return sum(w[g][..., None].astype(outs[g].dtype) * outs[g] for g in range(A_GROUPS))


def a_attend_prompt(z):
    B, T, _ = z.shape
    outs, lses = zip(*[dilated_attention_prompt(z, g) for g in range(A_GROUPS)])
    o = merge_groups([a.reshape(B * T, A_GROUP_W) for a in outs], [a.reshape(B * T, A_GROUP_W) for a in lses])
    _, k, v = a_split(z)
    states = [jnp.stack([k[:, :, g], v[:, :, g]], axis=2)[:, T - min(win, T):] for g, (win, _) in enumerate(A_PATTERNS)]
    return o.reshape(B, T, A_GROUP_W), states


def a_attend_sample(z, bufs):
    Bd, S, _ = z.shape
    q, k, v = a_split(z)
    ks, vs, rows, states = [], [], [], []
    for g in range(A_GROUPS):
        wb = bufs[g].shape[1]
        full = jnp.concatenate([bufs[g], jnp.stack([k[:, :, g], v[:, :, g]], axis=2)], axis=1)
        ks.append(full[:, :, 0])
        vs.append(full[:, :, 1])
        rows.append(wb + jnp.arange(S))
        states.append(full[:, S:])
    o = dilated_attend(q, ks, vs, rows).reshape(Bd, S, A_GROUP_W)
    return o, states


def nsa_split(z):
    B, T, _ = z.shape
    q = z[..., :NSA_Q].reshape(B, T, NSA_KV, NSA_G, NSA_HEAD_DIM)
    kv = z[..., NSA_Q:NSA_Q + 3 * NSA_KVW].reshape(B, T, 3, 2, NSA_KV, NSA_HEAD_DIM)
    gates = jax.nn.sigmoid(z[..., NSA_Q + 3 * NSA_KVW:].reshape(B, T, NSA_KV, NSA_G, 3))
    return q, kv[:, :, 0], kv[:, :, 1], kv[:, :, 2], gates


def cmp_chunk_proj(rows, w_cmp1):
    B, L = rows.shape[:2]
    n = B * (L // CMP_STRIDE)
    c = rows.reshape(n, CMP_STRIDE, 2, NSA_KV, NSA_HEAD_DIM)
    w = w_cmp1.reshape(2, CMP_BLOCK // CMP_STRIDE, CMP_STRIDE, NSA_HEAD_DIM, CMP_HIDDEN)
    outs = []
    for s in range(2):
        xs = c[:, :, s].transpose(0, 2, 1, 3).reshape(n * NSA_KV, CMP_STRIDE * NSA_HEAD_DIM)
        ws = w[s].transpose(1, 2, 0, 3).reshape(CMP_STRIDE * NSA_HEAD_DIM, 2 * CMP_HIDDEN)
        outs.append(matmul(xs, ws, 512, 2 * CMP_HIDDEN).reshape(B, L // CMP_STRIDE, NSA_KV, 2, CMP_HIDDEN))
    return jnp.stack(outs, axis=2)


def cmp_blocks(P, w_cmp1, w_cmp2, pe_cmp):
    bias = jnp.einsum('sjd,sjdf->sf', pe_cmp, w_cmp1, precision=lax.Precision.HIGHEST)
    hid = jax.nn.gelu(P[:, :-1, :, :, 0] + P[:, 1:, :, :, 1] + bias[:, None, :])
    return jnp.einsum('bcsnf,sfd->bcsnd', hid, w_cmp2)


def nsa_compressed_and_select(q, comp, qpos, total_len):
    n_cmp = comp.shape[1]
    end = CMP_STRIDE * jnp.arange(n_cmp) + CMP_BLOCK - 1
    mask = end[None, :] <= qpos[:, None]
    o_cmp, p = attend_shared(q, comp[:, :, 0], comp[:, :, 1], mask, NSA_HEAD_DIM ** -0.5)
    imp = jnp.sum(p, axis=2)
    n_sel = -(-total_len // SEL_BLOCK)
    r = SEL_BLOCK // CMP_STRIDE
    front = CMP_BLOCK // CMP_STRIDE - 1
    span = front + r
    back = max(0, r * (n_sel - 1) + span - (n_cmp + front))
    imp = jnp.pad(imp, ((0, 0), (0, 0), (0, 0), (front, back)))
    p_slc = sum(imp[..., o:o + r * (n_sel - 1) + 1:r] for o in range(span))
    tb = qpos // SEL_BLOCK
    j = jnp.arange(n_sel)[None, :]
    valid = j <= tb[:, None]
    forced = (j == 0) | (j == tb[:, None]) | (j == tb[:, None] - 1)
    score = jnp.where(valid, jnp.where(forced, FORCE_SCORE, p_slc), NEG_INF)
    top_s, idx = lax.top_k(score, min(SEL_TOPK, n_sel))
    return o_cmp, jnp.swapaxes(idx, 1, 2), jnp.swapaxes(top_s > 0.5 * NEG_INF, 1, 2)


def nsa_selected_attend(q_rot, ks, vs, idx, sel_valid, qpos):
    B, Tq, KV, K = idx.shape
    kpos = idx[..., None] * SEL_BLOCK + jnp.arange(SEL_BLOCK)
    mask = sel_valid[..., None] & (kpos <= qpos[:, None, None, None])
    n = K * SEL_BLOCK
    o, _ = attend_gathered(q_rot, ks.reshape(B, Tq, KV, n, NSA_HEAD_DIM), vs.reshape(B, Tq, KV, n, NSA_HEAD_DIM),
                           mask.reshape(B, Tq, KV, n), NSA_HEAD_DIM ** -0.5)
    return o


def nsa_merge(gates, o_cmp, o_sel, o_win):
    o = gates[..., 0:1] * o_cmp + gates[..., 1:2] * o_sel + gates[..., 2:3] * o_win
    B, T = o.shape[:2]
    return o.reshape(B, T, NSA_Q)


def nsa_attend_prompt(z, cos, sin, w_cmp1, w_cmp2, pe_cmp):
    B, T, _ = z.shape
    pos = jnp.arange(T)
    q, cmp_rows, sel_rows, win_rows, gates = nsa_split(z)
    comp = cmp_blocks(cmp_chunk_proj(cmp_rows, w_cmp1), w_cmp1, w_cmp2, pe_cmp)
    o_cmp, idx, sel_valid = nsa_compressed_and_select(q, comp, pos, T)
    n_sel = T // SEL_BLOCK
    picked = (idx[..., None] == jnp.arange(n_sel)) & sel_valid[..., None]
    member = jnp.swapaxes(jnp.any(picked, axis=3), 1, 2).astype(BF16)
    shape = (B, T, NSA_KV, NSA_G, NSA_HEAD_DIM)
    o_sel = nsa_selected_prompt(z, member, cos, sin).reshape(shape)
    o_win = nsa_window_prompt(z, cos, sin).reshape(shape)
    o = nsa_merge(gates, o_cmp, o_sel, o_win)
    return o, [cmp_rows, sel_rows, win_rows[:, T - min(NSA_WINDOW, T):]]


def nsa_attend_sample(z, cache_cmp, cache_sel, win_buf, page_table, w_cmp1, w_cmp2, pe_cmp):
    Bd, S, _ = z.shape
    pos = PAST_LEN + jnp.arange(S)
    q, cmp_rows, sel_rows, win_rows, gates = nsa_split(z)
    q_rot = rope(q, pos)
    P = cmp_chunk_proj_paged(cache_cmp, page_table, w_cmp1)
    n_new = S // CMP_STRIDE
    if n_new > 0:
        P = jnp.concatenate([P, cmp_chunk_proj(cmp_rows[:, :n_new * CMP_STRIDE], w_cmp1)], axis=1)
    comp = cmp_blocks(P, w_cmp1, w_cmp2, pe_cmp)
    o_cmp, idx, sel_valid = nsa_compressed_and_select(q, comp, pos, PAST_LEN + S)
    per_page = PAGE_SIZE // SEL_BLOCK
    nbp = PAST_LEN // SEL_BLOCK
    nbn = -(-S // SEL_BLOCK)
    pool = cache_sel.reshape(-1, SEL_BLOCK, 2, NSA_KV, NSA_HEAD_DIM)
    bi = jnp.arange(Bd)[:, None, None, None]
    hi = jnp.arange(NSA_KV)[None, None, :, None]
    jp = jnp.minimum(idx, nbp - 1)
    phys = page_table[bi, jp // per_page] * per_page + jp % per_page
    gp = pool.reshape(pool.shape[0], -1)[phys].reshape(phys.shape + pool.shape[1:])
    g_past = jnp.stack([gp[:, :, n, :, :, :, n, :] for n in range(NSA_KV)], axis=2)
    new_blk = jnp.pad(sel_rows, ((0, 0), (0, nbn * SEL_BLOCK - S), (0, 0), (0, 0), (0, 0)))
    new_blk = new_blk.reshape(Bd, nbn, SEL_BLOCK, 2, NSA_KV, NSA_HEAD_DIM).transpose(0, 4, 1, 2, 3, 5)
    g_new = new_blk[bi, hi, jnp.clip(idx - nbp, 0, nbn - 1)]
    g = jnp.where((idx >= nbp)[..., None, None, None], g_new, g_past)
    o_sel = nsa_selected_attend(q_rot, g[..., 0, :], g[..., 1, :], idx, sel_valid, pos)
    wb = win_buf.shape[1]
    full = jnp.concatenate([win_buf, win_rows], axis=1)
    kpos = PAST_LEN - wb + jnp.arange(wb + S)
    dist = pos[:, None] - kpos[None, :]
    mask = (dist >= 0) & (dist < NSA_WINDOW)
    o_win, _ = attend_shared(q_rot, full[:, :, 0], full[:, :, 1], mask, NSA_HEAD_DIM ** -0.5)
    o = nsa_merge(gates, o_cmp, o_sel, o_win)
    return o, [cmp_rows, sel_rows, full[:, S:]]


def mla_attend_sample(q_nope, q_rope, ckv, k_rope, cache_lat, cache_kr, page_table, w_kvb):
    Bd, S, _ = ckv.shape
    pos = PAST_LEN + jnp.arange(S)
    w_kv = w_kvb.reshape(KV_LORA, MLA_HEADS, QK_NOPE + V_DIM)
    q_abs = jnp.einsum('bshd,chd->bshc', q_nope, w_kv[..., :QK_NOPE])
    lat = cache_lat[page_table].reshape(Bd, PAST_LEN, KV_LORA)
    kr = cache_kr[page_table].reshape(Bd, PAST_LEN, QK_ROPE)

    def scores(c, r):
        return (jnp.einsum('bshc,btc->bhst', q_abs, c, preferred_element_type=F32)
                + jnp.einsum('bshr,btr->bhst', q_rope, r, preferred_element_type=F32)) * MLA_SCALE
    causal = pos[:, None] >= pos[None, :]
    s = jnp.concatenate([scores(lat, kr), jnp.where(causal, scores(ckv, k_rope), NEG_INF)], axis=-1)
    p = jax.nn.softmax(s, axis=-1).astype(lat.dtype)
    ctx = jnp.einsum('bhst,btc->bshc', p[..., :PAST_LEN], lat) + jnp.einsum('bhst,btc->bshc', p[..., PAST_LEN:], ckv)
    return jnp.einsum('bshc,chd->bshd', ctx, w_kv[..., QK_NOPE:]).reshape(Bd, S, MLA_HEADS * V_DIM)


def _split_tokens(z):
    return (z[:N_PROMPT].reshape(BATCH, SEQ, -1), z[N_PROMPT:].reshape(DEC_BATCH, DEC_SEQ, -1))


def _join_tokens(zp, zs):
    return jnp.concatenate([zp.reshape(N_PROMPT, -1), zs.reshape(N_SAMPLE, -1)], axis=0)


def kernel(x_prompt, x_sample, state_l0_a_g0, state_l0_a_g1, state_l0_a_g2, cache_l1_nsa_cmp, cache_l1_nsa_sel, state_l1_nsa_win, cache_l2_mla_latent, cache_l2_mla_krope, state_l3_a_g0, state_l3_a_g1, state_l3_a_g2, page_table, attn_norm, ffn_norm, final_norm, w_in_l0, w_out_l0, w_in_l1, w_cmp1_l1, w_cmp2_l1, pe_cmp_l1, w_out_l1, w_in_l2, q_norm_l2, kv_norm_l2, w_qb_l2, w_kvb_l2, w_out_l2, w_in_l3, w_out_l3, w_gu_l0, w_down_l0, router_l1, w_e_gu_l1, w_e_down_l1, w_gu_l2, w_down_l2, router_l3, w_e_gu_l3, w_e_down_l3):
    a_layers = {0: (w_in_l0, w_out_l0, (state_l0_a_g0, state_l0_a_g1, state_l0_a_g2)),
                3: (w_in_l3, w_out_l3, (state_l3_a_g0, state_l3_a_g1, state_l3_a_g2))}
    dense_layers = {0: (w_gu_l0, w_down_l0), 2: (w_gu_l2, w_down_l2)}
    moe_layers = {1: (router_l1, w_e_gu_l1, w_e_down_l1), 3: (router_l3, w_e_gu_l3, w_e_down_l3)}
    x = _join_tokens(x_prompt, x_sample)
    pos_p = jnp.arange(SEQ)
    pos_s = PAST_LEN + jnp.arange(DEC_SEQ)
    pos_all = jnp.concatenate([jnp.tile(pos_p, BATCH), jnp.tile(pos_s, DEC_BATCH)])
    cos_a, sin_a = rope_tables(pos_all, A_HEAD_DIM)
    cos_n, sin_n = rope_tables(pos_all, NSA_HEAD_DIM)
    a_rope = (cos_a, sin_a, A_HEAD_DIM, tuple(range(2 * A_GROUPS)))
    nsa_tn = NSA_KV * NSA_HEAD_DIM
    nsa_rope = (cos_n, sin_n, NSA_HEAD_DIM, ((NSA_Q + NSA_KVW) // nsa_tn, (NSA_Q + 2 * NSA_KVW) // nsa_tn))
    new_state = []
    for i in range(DEPTH):
        kind = i % N_MIXERS
        if kind == 0:
            w_in, w_out, bufs = a_layers[i]
            zp, zs = _split_tokens(norm_matmul(x, attn_norm[i], w_in, A_GROUP_W, a_rope))
            op, st_p = a_attend_prompt(zp)
            os_, st_s = a_attend_sample(zs, bufs)
        elif kind == 1:
            w_out = w_out_l1
            zp, zs = _split_tokens(norm_matmul(x, attn_norm[i], w_in_l1, nsa_tn, nsa_rope))
            op, st_p = nsa_attend_prompt(zp, cos_n[:SEQ], sin_n[:SEQ], w_cmp1_l1, w_cmp2_l1, pe_cmp_l1)
            os_, st_s = nsa_attend_sample(zs, cache_l1_nsa_cmp, cache_l1_nsa_sel, state_l1_nsa_win, page_table,
                                          w_cmp1_l1, w_cmp2_l1, pe_cmp_l1)
        else:
            w_out = w_out_l2
            z = norm_matmul(x, attn_norm[i], w_in_l2, MLA_IN)
            cq = rmsnorm(z[:, :Q_LORA], q_norm_l2)
            ckv = rmsnorm(z[:, Q_LORA:Q_LORA + KV_LORA], kv_norm_l2)
            w_qb = w_qb_l2.reshape(Q_LORA, MLA_HEADS, QK_NOPE + QK_ROPE)
            w_qb = jnp.concatenate([w_qb[..., :QK_NOPE].reshape(Q_LORA, MLA_HEADS * QK_NOPE),
                                    w_qb[..., QK_NOPE:].reshape(Q_LORA, MLA_HEADS * QK_ROPE)], axis=1)
            qh = matmul(cq, w_qb, ROW_TILE, 512)
            qhp, qhs = _split_tokens(qh)
            ckv_p, ckv_s = _split_tokens(ckv)
            kr_p, kr_s = _split_tokens(z[:, Q_LORA + KV_LORA:])
            kr_p, kr_s = rope(kr_p, pos_p), rope(kr_s, pos_s)
            kv_p = matmul(ckv_p.reshape(N_PROMPT, KV_LORA), w_kvb_l2, 512, 512).reshape(BATCH, SEQ, -1)
            op = mla_attention_prompt(qhp, kv_p, kr_p, cos_a[:SEQ], sin_a[:SEQ])
            qn_s = qhs[..., :MLA_HEADS * QK_NOPE].reshape(DEC_BATCH, DEC_SEQ, MLA_HEADS, QK_NOPE)
            qr_s = qhs[..., MLA_HEADS * QK_NOPE:].reshape(DEC_BATCH, DEC_SEQ, MLA_HEADS, QK_ROPE)
            os_ = mla_attend_sample(qn_s, rope(qr_s, pos_s), ckv_s, kr_s,
                                    cache_l2_mla_latent, cache_l2_mla_krope, page_table, w_kvb_l2)
            st_p, st_s = [ckv_p, kr_p], [ckv_s, kr_s]
        new_state += st_p + st_s
        x = matmul_residual(_join_tokens(op, os_), w_out, x)
        if i % 2 == 0:
            w_gu, w_down = dense_layers[i]
            x = dense_ffn(x, ffn_norm[i], w_gu, w_down)
        else:
            router, w_e_gu, w_e_down = moe_layers[i]
            x = moe_ffn(x, ffn_norm[i], router, w_e_gu, w_e_down)
    y = final_rmsnorm(x, final_norm)
    y_prompt, y_sample = _split_tokens(y)
    return (y_prompt, y_sample, *new_state)


def _final_norm_kernel(x_ref, g_ref, o_ref):
    o_ref[...] = _rms(x_ref[...], g_ref[...])


def final_rmsnorm(x, g):
    m, d = x.shape
    tm = ROW_TILE
    return pl.pallas_call(
        _final_norm_kernel,
        grid=(m // tm,),
        in_specs=[pl.BlockSpec((tm, d), lambda i: (i, 0)), pl.BlockSpec((1, d), lambda i: (0, 0))],
        out_specs=pl.BlockSpec((tm, d), lambda i: (i, 0)),
        out_shape=jax.ShapeDtypeStruct((m, d), F32),
        compiler_params=_params("parallel"),
        name="final_rmsnorm",
    )(x, g.reshape(1, d))
```

```python
import functools

import jax
import jax.numpy as jnp
from jax import lax
from jax.experimental import pallas as pl
from jax.experimental.pallas import tpu as pltpu

D_MODEL = 2048
BATCH = 2
SEQ = 4096
DEPTH = 4
DEC_BATCH = 128
DEC_SEQ = 4
PAST_LEN = 8192
PAGE_SIZE = 128

N_MIXERS = 3
RMS_EPS = 1e-6
ROPE_THETA = 10000.0
Q_BLOCK = 128
NEG_INF = -1e30
FORCE_SCORE = 1e9

A_PATTERNS = ((128, 1), (512, 4), (2048, 16))
A_GROUPS = len(A_PATTERNS)
A_HEADS = 8
A_HEAD_DIM = 64
A_GROUP_W = A_HEADS * A_HEAD_DIM
A_IN = 3 * A_GROUPS * A_GROUP_W

NSA_HEADS = 16
NSA_KV = 2
NSA_G = NSA_HEADS // NSA_KV
NSA_HEAD_DIM = 128
NSA_Q = NSA_HEADS * NSA_HEAD_DIM
NSA_KVW = 2 * NSA_KV * NSA_HEAD_DIM
NSA_IN = NSA_Q + 3 * NSA_KVW + 3 * NSA_HEADS
CMP_BLOCK = 32
CMP_STRIDE = 16
CMP_HIDDEN = 128
SEL_BLOCK = 64
SEL_TOPK = 16
NSA_WINDOW = 512

MLA_HEADS = 16
Q_LORA = 768
KV_LORA = 512
QK_NOPE = 128
QK_ROPE = 64
V_DIM = 128
MLA_IN = Q_LORA + KV_LORA + QK_ROPE
MLA_SCALE = (QK_NOPE + QK_ROPE) ** -0.5

D_FF = 5632
N_EXPERTS = 8
MOE_TOPK = 2

N_PROMPT = BATCH * SEQ
N_SAMPLE = DEC_BATCH * DEC_SEQ
N_TOK = N_PROMPT + N_SAMPLE

V7X_VMEM_LIMIT_BYTES = 56 * 1024 * 1024
LANES = 128

F32 = jnp.float32
BF16 = jnp.bfloat16

ROW_TILE = 544
FF_TILE = 512
MOE_ROW_TILE = 512
MOE_TILES = (MOE_TOPK * N_TOK) // MOE_ROW_TILE + N_EXPERTS


def _params(*semantics):
    return pltpu.CompilerParams(dimension_semantics=semantics, vmem_limit_bytes=V7X_VMEM_LIMIT_BYTES)


def _rms(x, g):
    return x * lax.rsqrt(jnp.mean(x * x, axis=-1, keepdims=True) + RMS_EPS) * g


def rope_tables(pos, head_dim):
    half = head_dim // 2
    inv = ROPE_THETA ** (-jnp.arange(half, dtype=F32) / half)
    ang = pos.astype(F32)[:, None] * inv[None, :]
    cos, sin = jnp.cos(ang), jnp.sin(ang)
    reps = LANES // head_dim
    return (jnp.tile(jnp.concatenate([cos, cos], axis=1), (1, reps)),
            jnp.tile(jnp.concatenate([-sin, sin], axis=1), (1, reps)))


def _rope_lanes(x, cos, sin, head_dim):
    n = x.shape[1]
    half = head_dim // 2
    reps = n // LANES
    if reps > 1:
        cos = jnp.concatenate([cos] * reps, axis=1)
        sin = jnp.concatenate([sin] * reps, axis=1)
    lane = lax.broadcasted_iota(jnp.int32, x.shape, 1) % head_dim
    partner = jnp.where(lane < half, pltpu.roll(x, n - half, 1), pltpu.roll(x, half, 1))
    return x * cos + partner * sin


def _norm_mm_kernel(x_ref, g_ref, w_ref, cos_ref, sin_ref, o_ref, h_ref, *, head_dim, rope_blocks):
    j = pl.program_id(1)

    @pl.when(j == 0)
    def _():
        h_ref[...] = _rms(x_ref[...], g_ref[...]).astype(BF16)

    acc = jnp.dot(h_ref[...], w_ref[...].astype(BF16), preferred_element_type=F32)
    if not rope_blocks:
        o_ref[...] = acc
        return
    is_rope = functools.reduce(jnp.logical_or, [j == b for b in rope_blocks])

    @pl.when(is_rope)
    def _():
        o_ref[...] = _rope_lanes(acc, cos_ref[...], sin_ref[...], head_dim)

    @pl.when(jnp.logical_not(is_rope))
    def _():
        o_ref[...] = acc


def norm_matmul(x, g, w, tn, rope=None):
    m, k = x.shape
    n = w.shape[1]
    tm = ROW_TILE
    if rope is None:
        cos = sin = jnp.zeros((m, LANES), F32)
        head_dim, rope_blocks = LANES, ()
    else:
        cos, sin, head_dim, rope_blocks = rope
    return pl.pallas_call(
        functools.partial(_norm_mm_kernel, head_dim=head_dim, rope_blocks=tuple(rope_blocks)),
        grid=(m // tm, pl.cdiv(n, tn)),
        in_specs=[pl.BlockSpec((tm, k), lambda i, j: (i, 0)),
                  pl.BlockSpec((1, k), lambda i, j: (0, 0)),
                  pl.BlockSpec((k, tn), lambda i, j: (0, j)),
                  pl.BlockSpec((tm, LANES), lambda i, j: (i, 0)),
                  pl.BlockSpec((tm, LANES), lambda i, j: (i, 0))],
        out_specs=pl.BlockSpec((tm, tn), lambda i, j: (i, j)),
        out_shape=jax.ShapeDtypeStruct((m, n), F32),
        scratch_shapes=[pltpu.VMEM((tm, k), BF16)],
        compiler_params=_params("parallel", "arbitrary"),
        name="norm_matmul",
    )(x, g.reshape(1, k), w, cos, sin)


def _mm_res_kernel(x_ref, w_ref, r_ref, o_ref):
    o_ref[...] = r_ref[...] + jnp.dot(x_ref[...].astype(BF16), w_ref[...].astype(BF16),
                                       preferred_element_type=F32)


def matmul_residual(x, w, res, tn=512):
    m, k = x.shape
    n = w.shape[1]
    tm = ROW_TILE
    return pl.pallas_call(
        _mm_res_kernel,
        grid=(m // tm, n // tn),
        in_specs=[pl.BlockSpec((tm, k), lambda i, j: (i, 0)),
                  pl.BlockSpec((k, tn), lambda i, j: (0, j)),
                  pl.BlockSpec((tm, tn), lambda i, j: (i, j))],
        out_specs=pl.BlockSpec((tm, tn), lambda i, j: (i, j)),
        out_shape=jax.ShapeDtypeStruct((m, n), F32),
        compiler_params=_params("parallel", "arbitrary"),
        name="matmul_residual",
    )(x, w, res)


def _mm_kernel(x_ref, w_ref, o_ref):
    o_ref[...] = jnp.dot(x_ref[...].astype(BF16), w_ref[...].astype(BF16), preferred_element_type=F32)


def matmul(x, w, tm, tn):
    m, k = x.shape
    n = w.shape[1]
    return pl.pallas_call(
        _mm_kernel,
        grid=(m // tm, n // tn),
        in_specs=[pl.BlockSpec((tm, k), lambda i, j: (i, 0)),
                  pl.BlockSpec((k, tn), lambda i, j: (0, j))],
        out_specs=pl.BlockSpec((tm, tn), lambda i, j: (i, j)),
        out_shape=jax.ShapeDtypeStruct((m, n), F32),
        compiler_params=_params("parallel", "arbitrary"),
        name="matmul",
    )(x, w)


def _ffn_kernel(x_ref, g_ref, wg_ref, wu_ref, wd_ref, o_ref, h_ref):
    @pl.when(pl.program_id(1) == 0)
    def _():
        x = x_ref[...]
        h_ref[...] = _rms(x, g_ref[...]).astype(BF16)
        o_ref[...] = x

    h = h_ref[...]
    gate = jnp.dot(h, wg_ref[...].astype(BF16), preferred_element_type=F32)
    up = jnp.dot(h, wu_ref[...].astype(BF16), preferred_element_type=F32)
    act = (gate * jax.nn.sigmoid(gate) * up).astype(BF16)
    o_ref[...] += jnp.dot(act, wd_ref[...].astype(BF16), preferred_element_type=F32)


def dense_ffn(x, g, w_gu, w_down):
    m, d = x.shape
    f = w_down.shape[0]
    tm, tf = ROW_TILE, FF_TILE
    nf = f // tf
    return pl.pallas_call(
        _ffn_kernel,
        grid=(m // tm, nf),
        in_specs=[pl.BlockSpec((tm, d), lambda i, j: (i, 0)),
                  pl.BlockSpec((1, d), lambda i, j: (0, 0)),
                  pl.BlockSpec((d, tf), lambda i, j: (0, j)),
                  pl.BlockSpec((d, tf), lambda i, j: (0, nf + j)),
                  pl.BlockSpec((tf, d), lambda i, j: (j, 0))],
        out_specs=pl.BlockSpec((tm, d), lambda i, j: (i, 0)),
        out_shape=jax.ShapeDtypeStruct((m, d), F32),
        scratch_shapes=[pltpu.VMEM((tm, d), BF16)],
        compiler_params=_params("parallel", "arbitrary"),
        name="dense_ffn",
    )(x, g.reshape(1, d), w_gu, w_gu, w_down)


def _norm_router_kernel(x_ref, g_ref, r_ref, h_ref, l_ref):
    h = _rms(x_ref[...], g_ref[...])
    h_ref[...] = h.astype(BF16)
    l_ref[...] = jnp.dot(h, r_ref[...], preferred_element_type=F32, precision=lax.Precision.HIGHEST)


def norm_router(x, g, router):
    m, d = x.shape
    tm = ROW_TILE
    r_pad = jnp.pad(router, ((0, 0), (0, LANES - router.shape[1])))
    return pl.pallas_call(
        _norm_router_kernel,
        grid=(m // tm,),
        in_specs=[pl.BlockSpec((tm, d), lambda i: (i, 0)),
                  pl.BlockSpec((1, d), lambda i: (0, 0)),
                  pl.BlockSpec((d, LANES), lambda i: (0, 0))],
        out_specs=[pl.BlockSpec((tm, d), lambda i: (i, 0)),
                   pl.BlockSpec((tm, LANES), lambda i: (i, 0))],
        out_shape=[jax.ShapeDtypeStruct((m, d), BF16), jax.ShapeDtypeStruct((m, LANES), F32)],
        compiler_params=_params("parallel"),
        name="norm_router",
    )(x, g.reshape(1, d), r_pad)


def _moe_kernel(te_ref, tv_ref, h_ref, gate_ref, wg_ref, wu_ref, wd_ref, o_ref):
    t = pl.program_id(0)
    j = pl.program_id(1)

    @pl.when(j == 0)
    def _():
        o_ref[...] = jnp.zeros_like(o_ref)

    @pl.when(tv_ref[t] == 1)
    def _():
        h = h_ref[...]
        gate = jnp.dot(h, wg_ref[...].astype(BF16), preferred_element_type=F32)
        up = jnp.dot(h, wu_ref[...].astype(BF16), preferred_element_type=F32)
        act = (gate * jax.nn.sigmoid(gate) * up).astype(BF16)
        o_ref[...] += jnp.dot(act, wd_ref[...].astype(BF16), preferred_element_type=F32)

    @pl.when(j == pl.num_programs(1) - 1)
    def _():
        o_ref[...] = o_ref[...] * gate_ref[...]


def moe_grouped_ffn(h_sorted, gate_sorted, tile_expert, tile_valid, w_e_gu, w_e_down):
    p, d = h_sorted.shape
    f = w_e_down.shape[1]
    tm, tf = MOE_ROW_TILE, FF_TILE
    nf = f // tf

    def frozen(j, tv, t):
        return jnp.where(tv[t] == 1, j, nf - 1)

    grid_spec = pltpu.PrefetchScalarGridSpec(
        num_scalar_prefetch=2,
        grid=(p // tm, nf),
        in_specs=[pl.BlockSpec((tm, d), lambda t, j, te, tv: (t, 0)),
                  pl.BlockSpec((tm, 1), lambda t, j, te, tv: (t, 0)),
                  pl.BlockSpec((None, d, tf), lambda t, j, te, tv: (te[t], 0, frozen(j, tv, t))),
                  pl.BlockSpec((None, d, tf), lambda t, j, te, tv: (te[t], 0, nf + frozen(j, tv, t))),
                  pl.BlockSpec((None, tf, d), lambda t, j, te, tv: (te[t], frozen(j, tv, t), 0))],
        out_specs=pl.BlockSpec((tm, d), lambda t, j, te, tv: (t, 0)),
    )
    return pl.pallas_call(
        _moe_kernel,
        grid_spec=grid_spec,
        out_shape=jax.ShapeDtypeStruct((p, d), F32),
        compiler_params=_params("arbitrary", "arbitrary"),
        name="moe_grouped_ffn",
    )(tile_expert, tile_valid, h_sorted, gate_sorted, w_e_gu, w_e_gu, w_e_down)


def moe_ffn(x, g, router, w_e_gu, w_e_down):
    m, d = x.shape
    tm = MOE_ROW_TILE
    h, logits = norm_router(x, g, router)
    top_v, top_i = lax.top_k(logits[:, :N_EXPERTS], MOE_TOPK)
    top_g = jax.nn.softmax(top_v, axis=-1)
    pair_e = top_i.reshape(-1)
    onehot = (pair_e[:, None] == jnp.arange(N_EXPERTS)[None, :]).astype(jnp.int32)
    rank = jnp.take_along_axis(jnp.cumsum(onehot, axis=0) - onehot, pair_e[:, None], axis=1)[:, 0]
    counts = jnp.sum(onehot, axis=0)
    tiles_e = (counts + tm - 1) // tm
    tile_end = jnp.cumsum(tiles_e)
    tile_start = tile_end - tiles_e
    slot = tile_start[pair_e] * tm + rank
    tile_ids = jnp.arange(MOE_TILES)
    tile_expert = jnp.minimum(jnp.sum(tile_ids[:, None] >= tile_end[None, :], axis=1), N_EXPERTS - 1)
    tile_valid = (tile_ids < tile_end[-1]).astype(jnp.int32)
    last_e = tile_expert[jnp.maximum(tile_end[-1] - 1, 0)]
    tile_expert = jnp.where(tile_valid == 1, tile_expert, last_e).astype(jnp.int32)
    n_slots = MOE_TILES * tm
    src = jnp.zeros((n_slots,), jnp.int32).at[slot].set(jnp.arange(MOE_TOPK * m, dtype=jnp.int32) // MOE_TOPK)
    gate_sorted = jnp.zeros((n_slots,), F32).at[slot].set(top_g.reshape(-1))
    h_sorted = h[src]
    y_sorted = moe_grouped_ffn(h_sorted, gate_sorted[:, None], tile_expert, tile_valid, w_e_gu, w_e_down)
    y = y_sorted[slot].reshape(m, MOE_TOPK, d)
    return x + y[:, 0] + y[:, 1]


A_TILE = 128


def _masked_softmax(s, mask):
    s = jnp.where(mask, s, NEG_INF)
    m = jnp.max(s, axis=-1, keepdims=True)
    e = jnp.where(mask, jnp.exp(s - m), 0.0)
    den = jnp.maximum(jnp.sum(e, axis=-1, keepdims=True), 1e-30)
    return e / den, m + jnp.log(den)


def _dilated_attn_kernel(q_ref, kp_ref, ko_ref, vp_ref, vo_ref, o_ref, lse_ref):
    t = pl.program_id(2)
    q = q_ref[0].astype(BF16)
    k = jnp.concatenate([kp_ref[0], ko_ref[0]], axis=0).astype(BF16)
    v = jnp.concatenate([vp_ref[0], vo_ref[0]], axis=0).astype(BF16)
    qi = lax.broadcasted_iota(jnp.int32, (A_TILE, 2 * A_TILE), 0)
    kc = lax.broadcasted_iota(jnp.int32, (A_TILE, 2 * A_TILE), 1)
    dist = qi + A_TILE - kc
    mask = (dist >= 0) & (dist <= A_TILE) & ((kc >= A_TILE) | (t > 0))
    outs, lses = [], []
    for h in range(A_HEADS):
        sl = slice(h * A_HEAD_DIM, (h + 1) * A_HEAD_DIM)
        s = lax.dot_general(q[:, sl], k[:, sl], (((1,), (1,)), ((), ())),
                            preferred_element_type=F32) * (A_HEAD_DIM ** -0.5)
        p, lse = _masked_softmax(s, mask)
        outs.append(jnp.dot(p.astype(BF16), v[:, sl], preferred_element_type=F32))
        lses.append(jnp.broadcast_to(lse, (A_TILE, A_HEAD_DIM)))
    o_ref[0] = jnp.concatenate(outs, axis=1)
    lse_ref[0] = jnp.concatenate(lses, axis=1)


def dilated_attention_prompt(z, g):
    B, T, _ = z.shape
    win, dil = A_PATTERNS[g]
    assert win // dil == A_TILE
    nblk = A_IN // A_GROUP_W
    zr = z.reshape(B, T // dil, dil * A_IN)
    w = A_GROUP_W

    def spec(section, prev):
        def index(b, r, t):
            return (b, jnp.maximum(t - 1, 0) if prev else t, r * nblk + section * A_GROUPS + g)
        return pl.BlockSpec((1, A_TILE, w), index)

    out_spec = pl.BlockSpec((1, A_TILE, w), lambda b, r, t: (b, t, r))
    o, lse = pl.pallas_call(
        _dilated_attn_kernel,
        grid=(B, dil, T // dil // A_TILE),
        in_specs=[spec(0, False), spec(1, True), spec(1, False), spec(2, True), spec(2, False)],
        out_specs=[out_spec, out_spec],
        out_shape=[jax.ShapeDtypeStruct((B, T // dil, dil * w), F32)] * 2,
        compiler_params=_params("parallel", "parallel", "arbitrary"),
        name="dilated_attention_prompt",
    )(zr, zr, zr, zr, zr)
    return o.reshape(B, T, w), lse.reshape(B, T, w)


def _merge_groups_kernel(o0, o1, o2, l0, l1, l2, out_ref):
    ls = [l0[...], l1[...], l2[...]]
    m = jnp.maximum(jnp.maximum(ls[0], ls[1]), ls[2])
    es = [jnp.exp(l - m) for l in ls]
    den = es[0] + es[1] + es[2]
    out_ref[...] = (es[0] / den) * o0[...] + (es[1] / den) * o1[...] + (es[2] / den) * o2[...]


def merge_groups(outs, lses):
    m, w = outs[0].shape
    tm = 512
    spec = pl.BlockSpec((tm, w), lambda i: (i, 0))
    return pl.pallas_call(
        _merge_groups_kernel,
        grid=(m // tm,),
        in_specs=[spec] * 6,
        out_specs=spec,
        out_shape=jax.ShapeDtypeStruct((m, w), F32),
        compiler_params=_params("parallel"),
        name="merge_groups",
    )(*outs, *lses)


NSA_KEY_TILE = 512
NSA_Q_COLS = NSA_G * NSA_HEAD_DIM


def _rope_q_heads(q, cos, sin):
    heads = []
    for g in range(NSA_G):
        qg = q[:, g * NSA_HEAD_DIM:(g + 1) * NSA_HEAD_DIM]
        heads.append((qg * cos + pltpu.roll(qg, NSA_HEAD_DIM // 2, 1) * sin).astype(BF16))
    return heads


def _nsa_sel_kernel(q_ref, cos_ref, sin_ref, mem_ref, k_ref, v_ref, o_ref, qs_ref, m_ref, l_ref, acc_ref):
    qt = pl.program_id(2)
    kt = pl.program_id(3)
    tq, tk, dh = Q_BLOCK, NSA_KEY_TILE, NSA_HEAD_DIM

    @pl.when(kt == 0)
    def _():
        for g, qg in enumerate(_rope_q_heads(q_ref[0], cos_ref[...], sin_ref[...])):
            qs_ref[g * tq:(g + 1) * tq, :] = qg
        m_ref[...] = jnp.full(m_ref.shape, NEG_INF, F32)
        l_ref[...] = jnp.zeros(l_ref.shape, F32)
        acc_ref[...] = jnp.zeros(acc_ref.shape, F32)

    @pl.when(kt * tk <= qt * tq + tq - 1)
    def _():
        k = k_ref[0].astype(BF16)
        v = v_ref[0].astype(BF16)
        n_blk = mem_ref.shape[-1]
        blk = lax.broadcasted_iota(jnp.int32, (n_blk, tk), 0)
        col = lax.broadcasted_iota(jnp.int32, (n_blk, tk), 1)
        expand = (blk == kt * (tk // SEL_BLOCK) + col // SEL_BLOCK).astype(BF16)
        picked = jnp.dot(mem_ref[0, 0], expand, preferred_element_type=F32) > 0.5
        qpos = qt * tq + lax.broadcasted_iota(jnp.int32, (tq, tk), 0)
        kpos = kt * tk + lax.broadcasted_iota(jnp.int32, (tq, tk), 1)
        mask = jnp.concatenate([picked & (kpos <= qpos)] * NSA_G, axis=0)
        s = lax.dot_general(qs_ref[...], k, (((1,), (1,)), ((), ())), preferred_element_type=F32) * (dh ** -0.5)
        s = jnp.where(mask, s, NEG_INF)
        m_prev = m_ref[...]
        m_new = jnp.maximum(m_prev, jnp.max(s, axis=-1, keepdims=True))
        alpha = jnp.exp(m_prev - m_new)
        e = jnp.where(mask, jnp.exp(s - m_new[:, :1]), 0.0)
        l_ref[...] = alpha * l_ref[...] + jnp.sum(e, axis=-1, keepdims=True)
        acc_ref[...] = alpha * acc_ref[...] + jnp.dot(e.astype(BF16), v, preferred_element_type=F32)
        m_ref[...] = m_new

    @pl.when(kt == pl.num_programs(3) - 1)
    def _():
        for g in range(NSA_G):
            rows = slice(g * tq, (g + 1) * tq)
            o_ref[0, :, g * dh:(g + 1) * dh] = acc_ref[rows, :] / jnp.maximum(l_ref[rows, :], 1e-30)


def nsa_selected_prompt(z, member, cos, sin):
    B, T, _ = z.shape
    tq, tk, dh = Q_BLOCK, NSA_KEY_TILE, NSA_HEAD_DIM
    k_blk = (NSA_Q + NSA_KVW) // dh
    v_blk = k_blk + NSA_KV

    def last_tile(qt):
        return (qt * tq + tq - 1) // tk

    return pl.pallas_call(
        _nsa_sel_kernel,
        grid=(B, NSA_KV, T // tq, T // tk),
        in_specs=[pl.BlockSpec((1, tq, NSA_Q_COLS), lambda b, n, qt, kt: (b, qt, n)),
                  pl.BlockSpec((tq, LANES), lambda b, n, qt, kt: (qt, 0)),
                  pl.BlockSpec((tq, LANES), lambda b, n, qt, kt: (qt, 0)),
                  pl.BlockSpec((1, 1, tq, T // SEL_BLOCK), lambda b, n, qt, kt: (b, n, qt, 0)),
                  pl.BlockSpec((1, tk, dh), lambda b, n, qt, kt: (b, jnp.minimum(kt, last_tile(qt)), k_blk + n)),
                  pl.BlockSpec((1, tk, dh), lambda b, n, qt, kt: (b, jnp.minimum(kt, last_tile(qt)), v_blk + n))],
        out_specs=pl.BlockSpec((1, tq, NSA_Q_COLS), lambda b, n, qt, kt: (b, qt, n)),
        out_shape=jax.ShapeDtypeStruct((B, T, NSA_Q), F32),
        scratch_shapes=[pltpu.VMEM((NSA_G * tq, dh), BF16), pltpu.VMEM((NSA_G * tq, LANES), F32),
                        pltpu.VMEM((NSA_G * tq, LANES), F32), pltpu.VMEM((NSA_G * tq, dh), F32)],
        compiler_params=_params("parallel", "parallel", "parallel", "arbitrary"),
        name="nsa_selected_prompt",
    )(z, cos, sin, member, z, z)


NSA_WIN_TILES = NSA_WINDOW // Q_BLOCK + 1


def _nsa_win_kernel(q_ref, cos_ref, sin_ref, *refs):
    k_refs, v_refs, o_ref = refs[:NSA_WIN_TILES], refs[NSA_WIN_TILES:2 * NSA_WIN_TILES], refs[-1]
    qt = pl.program_id(2)
    tq, dh = Q_BLOCK, NSA_HEAD_DIM
    nk = NSA_WIN_TILES * tq
    k = jnp.concatenate([r[0] for r in k_refs], axis=0).astype(BF16)
    v = jnp.concatenate([r[0] for r in v_refs], axis=0).astype(BF16)
    qpos = qt * tq + lax.broadcasted_iota(jnp.int32, (tq, nk), 0)
    kpos = (qt - (NSA_WIN_TILES - 1)) * tq + lax.broadcasted_iota(jnp.int32, (tq, nk), 1)
    dist = qpos - kpos
    mask = (dist >= 0) & (dist < NSA_WINDOW) & (kpos >= 0)
    for g, qg in enumerate(_rope_q_heads(q_ref[0], cos_ref[...], sin_ref[...])):
        s = lax.dot_general(qg, k, (((1,), (1,)), ((), ())), preferred_element_type=F32) * (dh ** -0.5)
        p, _ = _masked_softmax(s, mask)
        o_ref[0, :, g * dh:(g + 1) * dh] = jnp.dot(p.astype(BF16), v, preferred_element_type=F32)


def nsa_window_prompt(z, cos, sin):
    B, T, _ = z.shape
    tq, dh = Q_BLOCK, NSA_HEAD_DIM
    k_blk = (NSA_Q + 2 * NSA_KVW) // dh
    v_blk = k_blk + NSA_KV

    def kv_spec(first_blk, i):
        back = NSA_WIN_TILES - 1 - i
        return pl.BlockSpec((1, tq, dh), lambda b, n, qt: (b, jnp.maximum(qt - back, 0), first_blk + n))

    return pl.pallas_call(
        _nsa_win_kernel,
        grid=(B, NSA_KV, T // tq),
        in_specs=[pl.BlockSpec((1, tq, NSA_Q_COLS), lambda b, n, qt: (b, qt, n)),
                  pl.BlockSpec((tq, LANES), lambda b, n, qt: (qt, 0)),
                  pl.BlockSpec((tq, LANES), lambda b, n, qt: (qt, 0))]
                 + [kv_spec(k_blk, i) for i in range(NSA_WIN_TILES)]
                 + [kv_spec(v_blk, i) for i in range(NSA_WIN_TILES)],
        out_specs=pl.BlockSpec((1, tq, NSA_Q_COLS), lambda b, n, qt: (b, qt, n)),
        out_shape=jax.ShapeDtypeStruct((B, T, NSA_Q), F32),
        compiler_params=_params("parallel", "parallel", "arbitrary"),
        name="nsa_window_prompt",
    )(z, cos, sin, *([z] * (2 * NSA_WIN_TILES)))


N_PAGES = PAST_LEN // PAGE_SIZE
CHUNKS_PER_PAGE = PAGE_SIZE // CMP_STRIDE
PAGE_SECTIONS = 2 * NSA_KV


def _cmp_paged_kernel(pt_ref, *refs):
    page_refs, w_ref, o_ref = refs[:N_PAGES], refs[N_PAGES], refs[N_PAGES + 1]
    hc2 = 2 * CMP_HIDDEN
    for s in range(2):
        for n in range(NSA_KV):
            acc = jnp.zeros((N_PAGES * CHUNKS_PER_PAGE, hc2), F32)
            for j in range(CMP_STRIDE):
                first = j * PAGE_SECTIONS + s * NSA_KV + n
                x = jnp.concatenate([r[pl.ds(first, CHUNKS_PER_PAGE, stride=CMP_STRIDE * PAGE_SECTIONS), :]
                                     for r in page_refs], axis=0)
                acc = acc + jnp.dot(x.astype(BF16), w_ref[s, j].astype(BF16), preferred_element_type=F32)
            o_ref[0, :, (s * NSA_KV + n) * hc2:(s * NSA_KV + n + 1) * hc2] = acc


def cmp_chunk_proj_paged(cache_cmp, page_table, w_cmp1):
    n_phys = cache_cmp.shape[0]
    bd = page_table.shape[0]
    pages = cache_cmp.reshape(n_phys * PAGE_SIZE * PAGE_SECTIONS, NSA_HEAD_DIM)
    w = w_cmp1.reshape(2, CMP_BLOCK // CMP_STRIDE, CMP_STRIDE, NSA_HEAD_DIM, CMP_HIDDEN)
    w = w.transpose(0, 2, 3, 1, 4).reshape(2, CMP_STRIDE, NSA_HEAD_DIM, 2 * CMP_HIDDEN)
    n_chunks = N_PAGES * CHUNKS_PER_PAGE
    out_w = 2 * NSA_KV * 2 * CMP_HIDDEN

    def page_spec(i):
        return pl.BlockSpec((PAGE_SIZE * PAGE_SECTIONS, NSA_HEAD_DIM), lambda b, pt: (pt[b, i], 0))

    grid_spec = pltpu.PrefetchScalarGridSpec(
        num_scalar_prefetch=1,
        grid=(bd,),
        in_specs=[page_spec(i) for i in range(N_PAGES)]
                 + [pl.BlockSpec(w.shape, lambda b, pt: (0, 0, 0, 0))],
        out_specs=pl.BlockSpec((1, n_chunks, out_w), lambda b, pt: (b, 0, 0)),
    )
    out = pl.pallas_call(
        _cmp_paged_kernel,
        grid_spec=grid_spec,
        out_shape=jax.ShapeDtypeStruct((bd, n_chunks, out_w), F32),
        compiler_params=_params("arbitrary"),
        name="cmp_chunk_proj_paged",
    )(page_table, *([pages] * N_PAGES), w)
    return out.reshape(bd, n_chunks, 2, NSA_KV, 2, CMP_HIDDEN)


SUBLANES = 8
SEL_PAGES_PER_STEP = 2


def _nsa_sel_sample_kernel(pt_ref, q_ref, cos_ref, sin_ref, mpast_ref, mnew_ref, knew_ref, vnew_ref, *refs):
    page_refs, o_ref = refs[:N_PAGES], refs[N_PAGES]
    dh = NSA_HEAD_DIM
    rows = NSA_G * SUBLANES
    stride = PAGE_SECTIONS

    def step(q, k, v, mask, m, l, acc):
        s = lax.dot_general(q, k, (((1,), (1,)), ((), ())), preferred_element_type=F32) * (dh ** -0.5)
        s = jnp.where(mask, s, NEG_INF)
        m_new = jnp.maximum(m, jnp.max(s, axis=-1, keepdims=True))
        alpha = jnp.exp(m - m_new)
        e = jnp.where(mask, jnp.exp(s - m_new[:, :1]), 0.0)
        l = alpha * l + jnp.sum(e, axis=-1, keepdims=True)
        acc = alpha * acc + jnp.dot(e.astype(BF16), v, preferred_element_type=F32)
        return m_new, l, acc

    for n in range(NSA_KV):
        q = jnp.concatenate(_rope_q_heads(q_ref[0, :, n * NSA_Q_COLS:(n + 1) * NSA_Q_COLS],
                                          cos_ref[...], sin_ref[...]), axis=0)
        m = jnp.full((rows, LANES), NEG_INF, F32)
        l = jnp.zeros((rows, LANES), F32)
        acc = jnp.zeros((rows, dh), F32)
        for i in range(0, N_PAGES, SEL_PAGES_PER_STEP):
            pages = page_refs[i:i + SEL_PAGES_PER_STEP]
            k = jnp.concatenate([r[pl.ds(n, PAGE_SIZE, stride=stride), :] for r in pages], axis=0).astype(BF16)
            v = jnp.concatenate([r[pl.ds(NSA_KV + n, PAGE_SIZE, stride=stride), :] for r in pages],
                                axis=0).astype(BF16)
            mask = mpast_ref[0, n, :, i * PAGE_SIZE:(i + SEL_PAGES_PER_STEP) * PAGE_SIZE] > 0.5
            m, l, acc = step(q, k, v, jnp.concatenate([mask] * NSA_G, axis=0), m, l, acc)
        pad = jnp.zeros((LANES - SUBLANES, dh), F32)
        k = jnp.concatenate([knew_ref[0, n], pad], axis=0).astype(BF16)
        v = jnp.concatenate([vnew_ref[0, n], pad], axis=0).astype(BF16)
        mask = mnew_ref[0, n] > 0.5
        m, l, acc = step(q, k, v, jnp.concatenate([mask] * NSA_G, axis=0), m, l, acc)
        o = acc / jnp.maximum(l, 1e-30)
        for g in range(NSA_G):
            o_ref[0, :, n * NSA_Q_COLS + g * dh:n * NSA_Q_COLS + (g + 1) * dh] = o[g * SUBLANES:(g + 1) * SUBLANES]


def nsa_selected_sample(zs, member, cache_sel, page_table, cos, sin):
    bd, S, _ = zs.shape
    n_phys = cache_sel.shape[0]
    dh = NSA_HEAD_DIM
    nbp = PAST_LEN // SEL_BLOCK
    pad_s = ((0, 0), (0, 0), (0, SUBLANES - S), (0, 0))
    q8 = jnp.pad(zs[..., :NSA_Q], ((0, 0), (0, SUBLANES - S), (0, 0)))
    k_new = zs[..., NSA_Q + NSA_KVW:NSA_Q + NSA_KVW + NSA_KV * dh].reshape(bd, S, NSA_KV, dh)
    v_new = zs[..., NSA_Q + NSA_KVW + NSA_KV * dh:NSA_Q + 2 * NSA_KVW].reshape(bd, S, NSA_KV, dh)
    k_new = jnp.pad(jnp.swapaxes(k_new, 1, 2), pad_s)
    v_new = jnp.pad(jnp.swapaxes(v_new, 1, 2), pad_s)
    m_past = jnp.pad(jnp.repeat(member[..., :nbp], SEL_BLOCK, axis=-1), pad_s)
    causal = jnp.arange(LANES)[None, :] <= jnp.arange(S)[:, None]
    m_new = jnp.pad(member[..., nbp:nbp + 1] * causal.astype(F32), pad_s)
    cos8 = jnp.pad(cos, ((0, SUBLANES - S), (0, 0)))
    sin8 = jnp.pad(sin, ((0, SUBLANES - S), (0, 0)))
    pages = cache_sel.reshape(n_phys * PAGE_SIZE * PAGE_SECTIONS, dh)

    def page_spec(i):
        return pl.BlockSpec((PAGE_SIZE * PAGE_SECTIONS, dh), lambda b, pt: (pt[b, i], 0))

    def per_seq(shape):
        return pl.BlockSpec((1,) + shape, lambda b, pt: (b,) + (0,) * len(shape))

    grid_spec = pltpu.PrefetchScalarGridSpec(
        num_scalar_prefetch=1,
        grid=(bd,),
        in_specs=[per_seq((SUBLANES, NSA_Q)),
                  pl.BlockSpec((SUBLANES, LANES), lambda b, pt: (0, 0)),
                  pl.BlockSpec((SUBLANES, LANES), lambda b, pt: (0, 0)),
                  per_seq((NSA_KV, SUBLANES, PAST_LEN)), per_seq((NSA_KV, SUBLANES, LANES)),
                  per_seq((NSA_KV, SUBLANES, dh)), per_seq((NSA_KV, SUBLANES, dh))]
                 + [page_spec(i) for i in range(N_PAGES)],
        out_specs=per_seq((SUBLANES, NSA_Q)),
    )
    out = pl.pallas_call(
        _nsa_sel_sample_kernel,
        grid_spec=grid_spec,
        out_shape=jax.ShapeDtypeStruct((bd, SUBLANES, NSA_Q), F32),
        compiler_params=_params("arbitrary"),
        name="nsa_selected_sample",
    )(page_table, q8, cos8, sin8, m_past, m_new, k_new, v_new, *([pages] * N_PAGES))
    return out[:, :S]


MLA_TQ = 1024
MLA_TK = 512
MLA_PAIR = 2


def _mla_prompt_kernel(qn_ref, qr_ref, cos_ref, sin_ref, kv_ref, kr_ref, o_ref, qr_s, m_s, l_s, acc_s):
    qt = pl.program_id(2)
    kt = pl.program_id(3)
    tq, tk = MLA_TQ, MLA_TK

    @pl.when(kt == 0)
    def _():
        qr_s[...] = _rope_lanes(qr_ref[0], cos_ref[...], sin_ref[...], QK_ROPE).astype(BF16)
        m_s[...] = jnp.full(m_s.shape, NEG_INF, F32)
        l_s[...] = jnp.zeros(l_s.shape, F32)
        acc_s[...] = jnp.zeros(acc_s.shape, F32)

    @pl.when(kt * tk <= qt * tq + tq - 1)
    def _():
        kr = kr_ref[0].astype(BF16)
        qpos = qt * tq + lax.broadcasted_iota(jnp.int32, (tq, tk), 0)
        kpos = kt * tk + lax.broadcasted_iota(jnp.int32, (tq, tk), 1)
        mask = qpos >= kpos
        for h in range(MLA_PAIR):
            kn = kv_ref[0, :, h * (QK_NOPE + V_DIM):h * (QK_NOPE + V_DIM) + QK_NOPE].astype(BF16)
            v = kv_ref[0, :, h * (QK_NOPE + V_DIM) + QK_NOPE:(h + 1) * (QK_NOPE + V_DIM)].astype(BF16)
            qn = qn_ref[0, :, h * QK_NOPE:(h + 1) * QK_NOPE].astype(BF16)
            qr = qr_s[:, h * QK_ROPE:(h + 1) * QK_ROPE]
            dims = (((1,), (1,)), ((), ()))
            s = (lax.dot_general(qn, kn, dims, preferred_element_type=F32)
                 + lax.dot_general(qr, kr, dims, preferred_element_type=F32)) * MLA_SCALE
            s = jnp.where(mask, s, NEG_INF)
            m_prev = m_s[h]
            m_new = jnp.maximum(m_prev, jnp.max(s, axis=-1, keepdims=True))
            alpha = jnp.exp(m_prev - m_new)
            e = jnp.where(mask, jnp.exp(s - m_new[:, :1]), 0.0)
            l_s[h] = alpha * l_s[h] + jnp.sum(e, axis=-1, keepdims=True)
            acc_s[h] = alpha * acc_s[h] + jnp.dot(e.astype(BF16), v, preferred_element_type=F32)
            m_s[h] = m_new

    @pl.when(kt == pl.num_programs(3) - 1)
    def _():
        for h in range(MLA_PAIR):
            o_ref[0, :, h * V_DIM:(h + 1) * V_DIM] = acc_s[h] / jnp.maximum(l_s[h], 1e-30)


def mla_attention_prompt(qh, kv, k_rope, cos, sin):
    B, T, _ = kv.shape
    tq, tk = MLA_TQ, MLA_TK
    n_pairs = MLA_HEADS // MLA_PAIR
    rope_blk0 = MLA_HEADS * QK_NOPE // (MLA_PAIR * QK_ROPE)

    def last_tile(qt):
        return (qt * tq + tq - 1) // tk

    return pl.pallas_call(
        _mla_prompt_kernel,
        grid=(B, n_pairs, T // tq, T // tk),
        in_specs=[pl.BlockSpec((1, tq, MLA_PAIR * QK_NOPE), lambda b, p, qt, kt: (b, qt, p)),
                  pl.BlockSpec((1, tq, MLA_PAIR * QK_ROPE), lambda b, p, qt, kt: (b, qt, rope_blk0 + p)),
                  pl.BlockSpec((tq, LANES), lambda b, p, qt, kt: (qt, 0)),
                  pl.BlockSpec((tq, LANES), lambda b, p, qt, kt: (qt, 0)),
                  pl.BlockSpec((1, tk, MLA_PAIR * (QK_NOPE + V_DIM)),
                               lambda b, p, qt, kt: (b, jnp.minimum(kt, last_tile(qt)), p)),
                  pl.BlockSpec((1, tk, QK_ROPE), lambda b, p, qt, kt: (b, jnp.minimum(kt, last_tile(qt)), 0))],
        out_specs=pl.BlockSpec((1, tq, MLA_PAIR * V_DIM), lambda b, p, qt, kt: (b, qt, p)),
        out_shape=jax.ShapeDtypeStruct((B, T, MLA_HEADS * V_DIM), F32),
        scratch_shapes=[pltpu.VMEM((tq, MLA_PAIR * QK_ROPE), BF16), pltpu.VMEM((MLA_PAIR, tq, LANES), F32),
                        pltpu.VMEM((MLA_PAIR, tq, LANES), F32), pltpu.VMEM((MLA_PAIR, tq, V_DIM), F32)],
        compiler_params=_params("parallel", "parallel", "parallel", "arbitrary"),
        name="mla_attention_prompt",
    )(qh, qh, cos, sin, kv, k_rope)


def rope(x, pos):
    half = x.shape[-1] // 2
    inv = ROPE_THETA ** (-jnp.arange(half, dtype=F32) / half)
    ang = pos.astype(F32)[:, None] * inv[None, :]
    shape = (pos.shape[0],) + (1,) * (x.ndim - 3) + (half,)
    cos = jnp.cos(ang).reshape(shape)
    sin = jnp.sin(ang).reshape(shape)
    x1, x2 = x[..., :half], x[..., half:]
    return jnp.concatenate([x1 * cos - x2 * sin, x2 * cos + x1 * sin], axis=-1)


def rmsnorm(x, g):
    return _rms(x, g)


def attend_shared(q, k, v, mask, scale):
    s = jnp.einsum('bqkgd,btkd->bkgqt', q, k, preferred_element_type=F32) * scale
    s = jnp.where(mask, s, NEG_INF)
    m = jnp.max(s, axis=-1, keepdims=True)
    e = jnp.where(mask, jnp.exp(s - m), 0.0)
    p = e / jnp.maximum(jnp.sum(e, axis=-1, keepdims=True), 1e-30)
    o = jnp.einsum('bkgqt,btkd->bqkgd', p.astype(v.dtype), v)
    return o, p


def a_split(z):
    B, T, _ = z.shape
    z = z.reshape(B, T, 3, A_GROUPS, A_HEADS, A_HEAD_DIM)
    return z[:, :, 0], z[:, :, 1], z[:, :, 2]


def a_attend_prompt(z):
    B, T, _ = z.shape
    outs, lses = zip(*[dilated_attention_prompt(z, g) for g in range(A_GROUPS)])
    o = merge_groups([a.reshape(B * T, A_GROUP_W) for a in outs], [a.reshape(B * T, A_GROUP_W) for a in lses])
    _, k, v = a_split(z)
    states = [jnp.stack([k[:, :, g], v[:, :, g]], axis=2)[:, T - min(win, T):] for g, (win, _) in enumerate(A_PATTERNS)]
    return o.reshape(B, T, A_GROUP_W), states


def a_attend_sample(z, bufs):
    Bd, S, _ = z.shape
    q, k, v = a_split(z)
    outs, lses, states = [], [], []
    for g, (win, dil) in enumerate(A_PATTERNS):
        buf = bufs[g]
        assert buf.shape[1] == win
        new = jnp.stack([k[:, :, g], v[:, :, g]], axis=2)
        states.append(jnp.concatenate([buf[:, S:], new], axis=1))
        o_g, l_g = [], []
        for s in range(S):
            rows = jnp.concatenate([buf[:, s::dil], new[:, s % dil:s + 1:dil]], axis=1)
            sc = jnp.einsum('bhd,bnhd->bhn', q[:, s, g], rows[:, :, 0],
                            preferred_element_type=F32) * A_HEAD_DIM ** -0.5
            m = jnp.max(sc, axis=-1, keepdims=True)
            e = jnp.exp(sc - m)
            den = jnp.maximum(jnp.sum(e, axis=-1, keepdims=True), 1e-30)
            o_g.append(jnp.einsum('bhn,bnhd->bhd', e / den, rows[:, :, 1]))
            l_g.append((m + jnp.log(den))[..., 0])
        outs.append(jnp.stack(o_g, axis=1))
        lses.append(jnp.stack(l_g, axis=1))
    w = jax.nn.softmax(jnp.stack(lses, axis=0), axis=0)
    o = sum(w[g][..., None] * outs[g] for g in range(A_GROUPS))
    return o.reshape(Bd, S, A_GROUP_W), states


def nsa_split(z):
    B, T, _ = z.shape
    q = z[..., :NSA_Q].reshape(B, T, NSA_KV, NSA_G, NSA_HEAD_DIM)
    kv = z[..., NSA_Q:NSA_Q + 3 * NSA_KVW].reshape(B, T, 3, 2, NSA_KV, NSA_HEAD_DIM)
    gates = jax.nn.sigmoid(z[..., NSA_Q + 3 * NSA_KVW:].reshape(B, T, NSA_KV, NSA_G, 3))
    return q, kv[:, :, 0], kv[:, :, 1], kv[:, :, 2], gates


def cmp_chunk_proj(rows, w_cmp1):
    B, L = rows.shape[:2]
    n = B * (L // CMP_STRIDE)
    c = rows.reshape(n, CMP_STRIDE, 2, NSA_KV, NSA_HEAD_DIM)
    w = w_cmp1.reshape(2, CMP_BLOCK // CMP_STRIDE, CMP_STRIDE, NSA_HEAD_DIM, CMP_HIDDEN)
    outs = []
    for s in range(2):
        xs = c[:, :, s].transpose(0, 2, 1, 3).reshape(n * NSA_KV, CMP_STRIDE * NSA_HEAD_DIM)
        ws = w[s].transpose(1, 2, 0, 3).reshape(CMP_STRIDE * NSA_HEAD_DIM, 2 * CMP_HIDDEN)
        tm = min(512, n * NSA_KV)
        assert (n * NSA_KV) % tm == 0
        outs.append(matmul(xs, ws, tm, 2 * CMP_HIDDEN).reshape(B, L // CMP_STRIDE, NSA_KV, 2, CMP_HIDDEN))
    return jnp.stack(outs, axis=2)


def cmp_blocks(P, w_cmp1, w_cmp2, pe_cmp):
    bias = jnp.einsum('sjd,sjdf->sf', pe_cmp, w_cmp1, precision=lax.Precision.HIGHEST)
    hid = jax.nn.gelu(P[:, :-1, :, :, 0] + P[:, 1:, :, :, 1] + bias[:, None, :])
    return jnp.einsum('bcsnf,sfd->bcsnd', hid, w_cmp2)


def nsa_compressed_and_select(q, comp, qpos, total_len):
    n_cmp = comp.shape[1]
    end = CMP_STRIDE * jnp.arange(n_cmp) + CMP_BLOCK - 1
    mask = end[None, :] <= qpos[:, None]
    o_cmp, p = attend_shared(q, comp[:, :, 0], comp[:, :, 1], mask, NSA_HEAD_DIM ** -0.5)
    imp = jnp.sum(p, axis=2)
    n_sel = -(-total_len // SEL_BLOCK)
    r = SEL_BLOCK // CMP_STRIDE
    front = CMP_BLOCK // CMP_STRIDE - 1
    span = front + r
    back = max(0, r * (n_sel - 1) + span - (n_cmp + front))
    imp = jnp.pad(imp, ((0, 0), (0, 0), (0, 0), (front, back)))
    p_slc = sum(imp[..., o:o + r * (n_sel - 1) + 1:r] for o in range(span))
    tb = qpos // SEL_BLOCK
    j = jnp.arange(n_sel)[None, :]
    valid = j <= tb[:, None]
    forced = (j == 0) | (j == tb[:, None]) | (j == tb[:, None] - 1)
    score = jnp.where(valid, jnp.where(forced, FORCE_SCORE, p_slc), NEG_INF)
    block = jnp.arange(n_sel)
    beats = (score[..., None, :] > score[..., :, None]) | (
        (score[..., None, :] == score[..., :, None]) & (block[None, :] < block[:, None]))
    member = (jnp.sum(beats, axis=-1) < min(SEL_TOPK, n_sel)) & (score > 0.5 * NEG_INF)
    return o_cmp, member


def nsa_merge(gates, o_cmp, o_sel, o_win):
    o = gates[..., 0:1] * o_cmp + gates[..., 1:2] * o_sel + gates[..., 2:3] * o_win
    B, T = o.shape[:2]
    return o.reshape(B, T, NSA_Q)


def nsa_attend_prompt(z, cos, sin, w_cmp1, w_cmp2, pe_cmp):
    B, T, _ = z.shape
    pos = jnp.arange(T)
    q, cmp_rows, sel_rows, win_rows, gates = nsa_split(z)
    comp = cmp_blocks(cmp_chunk_proj(cmp_rows, w_cmp1), w_cmp1, w_cmp2, pe_cmp)
    o_cmp, member = nsa_compressed_and_select(q, comp, pos, T)
    shape = (B, T, NSA_KV, NSA_G, NSA_HEAD_DIM)
    o_sel = nsa_selected_prompt(z, member.astype(BF16), cos, sin).reshape(shape)
    o_win = nsa_window_prompt(z, cos, sin).reshape(shape)
    o = nsa_merge(gates, o_cmp, o_sel, o_win)
    return o, [cmp_rows, sel_rows, win_rows[:, T - min(NSA_WINDOW, T):]]


def nsa_attend_sample(z, cos, sin, cache_cmp, cache_sel, win_buf, page_table, w_cmp1, w_cmp2, pe_cmp):
    Bd, S, _ = z.shape
    pos = PAST_LEN + jnp.arange(S)
    q, cmp_rows, sel_rows, win_rows, gates = nsa_split(z)
    q_rot = rope(q, pos)
    P = cmp_chunk_proj_paged(cache_cmp, page_table, w_cmp1)
    n_new = S // CMP_STRIDE
    if n_new > 0:
        P = jnp.concatenate([P, cmp_chunk_proj(cmp_rows[:, :n_new * CMP_STRIDE], w_cmp1)], axis=1)
    comp = cmp_blocks(P, w_cmp1, w_cmp2, pe_cmp)
    o_cmp, member = nsa_compressed_and_select(q, comp, pos, PAST_LEN + S)
    assert S <= SEL_BLOCK and PAST_LEN % SEL_BLOCK == 0
    o_sel = nsa_selected_sample(z, member.astype(F32), cache_sel, page_table, cos, sin)
    o_sel = o_sel.reshape(Bd, S, NSA_KV, NSA_G, NSA_HEAD_DIM)
    wb = win_buf.shape[1]
    full = jnp.concatenate([win_buf, win_rows], axis=1)
    kpos = PAST_LEN - wb + jnp.arange(wb + S)
    dist = pos[:, None] - kpos[None, :]
    mask = (dist >= 0) & (dist < NSA_WINDOW)
    o_win, _ = attend_shared(q_rot, full[:, :, 0], full[:, :, 1], mask, NSA_HEAD_DIM ** -0.5)
    o = nsa_merge(gates, o_cmp, o_sel, o_win)
    return o, [cmp_rows, sel_rows, full[:, S:]]


def mla_attend_sample(q_nope, q_rope, ckv, k_rope, cache_lat, cache_kr, page_table, w_kvb):
    Bd, S, _ = ckv.shape
    pos = PAST_LEN + jnp.arange(S)
    w_kv = w_kvb.reshape(KV_LORA, MLA_HEADS, QK_NOPE + V_DIM)
    q_abs = jnp.einsum('bshd,chd->bshc', q_nope, w_kv[..., :QK_NOPE])
    lat = cache_lat[page_table].reshape(Bd, PAST_LEN, KV_LORA)
    kr = cache_kr[page_table].reshape(Bd, PAST_LEN, QK_ROPE)

    def scores(c, r):
        return (jnp.einsum('bshc,btc->bhst', q_abs, c, preferred_element_type=F32)
                + jnp.einsum('bshr,btr->bhst', q_rope, r, preferred_element_type=F32)) * MLA_SCALE
    causal = pos[:, None] >= pos[None, :]
    s = jnp.concatenate([scores(lat, kr), jnp.where(causal, scores(ckv, k_rope), NEG_INF)], axis=-1)
    p = jax.nn.softmax(s, axis=-1).astype(lat.dtype)
    ctx = jnp.einsum('bhst,btc->bshc', p[..., :PAST_LEN], lat) + jnp.einsum('bhst,btc->bshc', p[..., PAST_LEN:], ckv)
    return jnp.einsum('bshc,chd->bshd', ctx, w_kv[..., QK_NOPE:]).reshape(Bd, S, MLA_HEADS * V_DIM)


def _split_tokens(z):
    return (z[:N_PROMPT].reshape(BATCH, SEQ, -1), z[N_PROMPT:].reshape(DEC_BATCH, DEC_SEQ, -1))


def _join_tokens(zp, zs):
    return jnp.concatenate([zp.reshape(N_PROMPT, -1), zs.reshape(N_SAMPLE, -1)], axis=0)


def kernel(x_prompt, x_sample, state_l0_a_g0, state_l0_a_g1, state_l0_a_g2, cache_l1_nsa_cmp, cache_l1_nsa_sel, state_l1_nsa_win, cache_l2_mla_latent, cache_l2_mla_krope, state_l3_a_g0, state_l3_a_g1, state_l3_a_g2, page_table, attn_norm, ffn_norm, final_norm, w_in_l0, w_out_l0, w_in_l1, w_cmp1_l1, w_cmp2_l1, pe_cmp_l1, w_out_l1, w_in_l2, q_norm_l2, kv_norm_l2, w_qb_l2, w_kvb_l2, w_out_l2, w_in_l3, w_out_l3, w_gu_l0, w_down_l0, router_l1, w_e_gu_l1, w_e_down_l1, w_gu_l2, w_down_l2, router_l3, w_e_gu_l3, w_e_down_l3):
    a_layers = {0: (w_in_l0, w_out_l0, (state_l0_a_g0, state_l0_a_g1, state_l0_a_g2)),
                3: (w_in_l3, w_out_l3, (state_l3_a_g0, state_l3_a_g1, state_l3_a_g2))}
    dense_layers = {0: (w_gu_l0, w_down_l0), 2: (w_gu_l2, w_down_l2)}
    moe_layers = {1: (router_l1, w_e_gu_l1, w_e_down_l1), 3: (router_l3, w_e_gu_l3, w_e_down_l3)}
    x = _join_tokens(x_prompt, x_sample)
    pos_p = jnp.arange(SEQ)
    pos_s = PAST_LEN + jnp.arange(DEC_SEQ)
    pos_all = jnp.concatenate([jnp.tile(pos_p, BATCH), jnp.tile(pos_s, DEC_BATCH)])
    cos_a, sin_a = rope_tables(pos_all, A_HEAD_DIM)
    cos_n, sin_n = rope_tables(pos_all, NSA_HEAD_DIM)
    a_rope = (cos_a, sin_a, A_HEAD_DIM, tuple(range(2 * A_GROUPS)))
    nsa_tn = NSA_KV * NSA_HEAD_DIM
    nsa_rope = (cos_n, sin_n, NSA_HEAD_DIM, ((NSA_Q + NSA_KVW) // nsa_tn, (NSA_Q + 2 * NSA_KVW) // nsa_tn))
    new_state = []
    for i in range(DEPTH):
        kind = i % N_MIXERS
        if kind == 0:
            w_in, w_out, bufs = a_layers[i]
            zp, zs = _split_tokens(norm_matmul(x, attn_norm[i], w_in, A_GROUP_W, a_rope))
            op, st_p = a_attend_prompt(zp)
            os_, st_s = a_attend_sample(zs, bufs)
        elif kind == 1:
            w_out = w_out_l1
            zp, zs = _split_tokens(norm_matmul(x, attn_norm[i], w_in_l1, nsa_tn, nsa_rope))
            op, st_p = nsa_attend_prompt(zp, cos_n[:SEQ], sin_n[:SEQ], w_cmp1_l1, w_cmp2_l1, pe_cmp_l1)
            os_, st_s = nsa_attend_sample(zs, cos_n[N_PROMPT:N_PROMPT + DEC_SEQ], sin_n[N_PROMPT:N_PROMPT + DEC_SEQ],
                                          cache_l1_nsa_cmp, cache_l1_nsa_sel, state_l1_nsa_win, page_table,
                                          w_cmp1_l1, w_cmp2_l1, pe_cmp_l1)
        else:
            w_out = w_out_l2
            z = norm_matmul(x, attn_norm[i], w_in_l2, MLA_IN)
            cq = rmsnorm(z[:, :Q_LORA], q_norm_l2)
            ckv = rmsnorm(z[:, Q_LORA:Q_LORA + KV_LORA], kv_norm_l2)
            w_qb = w_qb_l2.reshape(Q_LORA, MLA_HEADS, QK_NOPE + QK_ROPE)
            w_qb = jnp.concatenate([w_qb[..., :QK_NOPE].reshape(Q_LORA, MLA_HEADS * QK_NOPE),
                                    w_qb[..., QK_NOPE:].reshape(Q_LORA, MLA_HEADS * QK_ROPE)], axis=1)
            qh = matmul(cq, w_qb, ROW_TILE, 512)
            qhp, qhs = _split_tokens(qh)
            ckv_p, ckv_s = _split_tokens(ckv)
            kr_p, kr_s = _split_tokens(z[:, Q_LORA + KV_LORA:])
            kr_p, kr_s = rope(kr_p, pos_p), rope(kr_s, pos_s)
            kv_p = matmul(ckv_p.reshape(N_PROMPT, KV_LORA), w_kvb_l2, 512, 512).reshape(BATCH, SEQ, -1)
            op = mla_attention_prompt(qhp, kv_p, kr_p, cos_a[:SEQ], sin_a[:SEQ])
            qn_s = qhs[..., :MLA_HEADS * QK_NOPE].reshape(DEC_BATCH, DEC_SEQ, MLA_HEADS, QK_NOPE)
            qr_s = qhs[..., MLA_HEADS * QK_NOPE:].reshape(DEC_BATCH, DEC_SEQ, MLA_HEADS, QK_ROPE)
            os_ = mla_attend_sample(qn_s, rope(qr_s, pos_s), ckv_s, kr_s,
                                    cache_l2_mla_latent, cache_l2_mla_krope, page_table, w_kvb_l2)
            st_p, st_s = [ckv_p, kr_p], [ckv_s, kr_s]
        new_state += st_p + st_s
        x = matmul_residual(_join_tokens(op, os_), w_out, x)
        if i % 2 == 0:
            w_gu, w_down = dense_layers[i]
            x = dense_ffn(x, ffn_norm[i], w_gu, w_down)
        else:
            router, w_e_gu, w_e_down = moe_layers[i]
            x = moe_ffn(x, ffn_norm[i], router, w_e_gu, w_e_down)
    y = final_rmsnorm(x, final_norm)
    y_prompt, y_sample = _split_tokens(y)
    return (y_prompt, y_sample, *new_state)


def _final_norm_kernel(x_ref, g_ref, o_ref):
    o_ref[...] = _rms(x_ref[...], g_ref[...])


def final_rmsnorm(x, g):
    m, d = x.shape
    tm = ROW_TILE
    return pl.pallas_call(
        _final_norm_kernel,
        grid=(m // tm,),
        in_specs=[pl.BlockSpec((tm, d), lambda i: (i, 0)), pl.BlockSpec((1, d), lambda i: (0, 0))],
        out_specs=pl.BlockSpec((tm, d), lambda i: (i, 0)),
        out_shape=jax.ShapeDtypeStruct((m, d), F32),
        compiler_params=_params("parallel"),
        name="final_rmsnorm",
    )(x, g.reshape(1, d))
```
